```python
import math
import jax, jax.numpy as jnp
from jax import lax
import numpy as np

D_MODEL = 1024
BATCH = 4
SEQ = 8192
DEPTH = 1

CHUNK = 64
Q_BLOCK = 128
ATT_WIDTH = D_MODEL // 2
N_DIFF_HEADS = 4
DIFF_HEAD_DIM = ATT_WIDTH // (2 * N_DIFF_HEADS)
GMLP_WIDTH = D_MODEL - ATT_WIDTH
N_GMLP_GROUPS = 4
GMLP_GROUP_DIM = GMLP_WIDTH // N_GMLP_GROUPS
GMLP_CHUNK = 128
IN_WIDTH = 3 * ATT_WIDTH + 2 * GMLP_WIDTH
D_FF = 2816
CONV_WIDTH = 3
ROPE_THETA = 10000.0
LN_EPS = 1e-5
DEEPNORM_ALPHA = (2 * DEPTH) ** 0.25
DEEPNORM_BETA = (8 * DEPTH) ** -0.25

kernel_name = "hybrid_diffattn_gmlp_convffn_deepnorm"


def layer_norm(x, g, b):
    xf = x.astype(jnp.float32)
    mu = jnp.mean(xf, axis=-1, keepdims=True)
    var = jnp.mean(jnp.square(xf - mu), axis=-1, keepdims=True)
    y = (xf - mu) * lax.rsqrt(var + LN_EPS) * g.astype(jnp.float32) + b.astype(jnp.float32)
    return y.astype(x.dtype)


def rms_norm(x, g):
    xf = x.astype(jnp.float32)
    y = xf * lax.rsqrt(jnp.mean(jnp.square(xf), axis=-1, keepdims=True) + LN_EPS) * g.astype(jnp.float32)
    return y.astype(x.dtype)


def rope_tables(seq_len):
    pos = jnp.arange(seq_len, dtype=jnp.float32)
    inv_freq = 1.0 / (ROPE_THETA ** (jnp.arange(0, DIFF_HEAD_DIM, 2, dtype=jnp.float32) / DIFF_HEAD_DIM))
    ang = pos[:, None] * inv_freq[None, :]
    ang = jnp.concatenate([ang, ang], axis=-1)
    return jnp.cos(ang), jnp.sin(ang)


def apply_rope(t, cos, sin):
    half = DIFF_HEAD_DIM // 2
    t1, t2 = t[..., :half], t[..., half:]
    rot = jnp.concatenate([-t2, t1], axis=-1)
    out = t.astype(jnp.float32) * cos[None, :, None, :] + rot.astype(jnp.float32) * sin[None, :, None, :]
    return out.astype(t.dtype)


def diff_attention(q, k, v, lam, cos, sin):
    B, S = q.shape[0], q.shape[1]
    n_blk = S // Q_BLOCK
    q = apply_rope(q, cos, sin) * (DIFF_HEAD_DIM ** -0.5)
    k = apply_rope(k, cos, sin)
    q_blocks = q.reshape(B, n_blk, Q_BLOCK, 2 * N_DIFF_HEADS, DIFF_HEAD_DIM).transpose(1, 0, 3, 2, 4)
    kt = k.transpose(0, 2, 1, 3)
    vt = v.transpose(0, 2, 1, 3)
    k_chunk = jnp.arange(S, dtype=jnp.int32) // CHUNK
    q_chunk = k_chunk.reshape(n_blk, Q_BLOCK)

    def one_block(args):
        q_blk, qc = args
        s = jnp.einsum('bhqd,bhkd->bhqk', q_blk, kt, preferred_element_type=jnp.float32)
        mask = k_chunk[None, :] <= qc[:, None]
        p = jax.nn.softmax(jnp.where(mask, s, -jnp.inf), axis=-1)
        p = p.reshape(B, N_DIFF_HEADS, 2, Q_BLOCK, S)
        a = (p[:, :, 0] - lam * p[:, :, 1]).astype(vt.dtype)
        return jnp.einsum('bhqk,bhkd->bhqd', a, vt)

    o = lax.map(one_block, (q_blocks, q_chunk))
    return o.transpose(1, 0, 3, 2, 4).reshape(B, S, N_DIFF_HEADS, 2 * DIFF_HEAD_DIM)


def spatial_gating(z, ln_g, ln_b, w_s, b_s):
    B, S = z.shape[0], z.shape[1]
    n_c = S // GMLP_CHUNK
    u, vg = z[..., :GMLP_WIDTH], z[..., GMLP_WIDTH:]
    vg = vg.reshape(B, S, N_GMLP_GROUPS, GMLP_GROUP_DIM)
    vg = layer_norm(vg, ln_g.reshape(N_GMLP_GROUPS, GMLP_GROUP_DIM), ln_b.reshape(N_GMLP_GROUPS, GMLP_GROUP_DIM))
    vg = vg.reshape(B, n_c, GMLP_CHUNK, N_GMLP_GROUPS, GMLP_GROUP_DIM)
    w_causal = jnp.tril(w_s)
    gate = jnp.einsum('gts,bcsgd->bctgd', w_causal, vg) + b_s.T[None, None, :, :, None]
    u = u.reshape(B, n_c, GMLP_CHUNK, N_GMLP_GROUPS, GMLP_GROUP_DIM)
    return (u * gate).reshape(B, S, GMLP_WIDTH)


def causal_dwconv(h, w, b):
    S = h.shape[1]
    hp = jnp.pad(h, ((0, 0), (CONV_WIDTH - 1, 0), (0, 0)))
    y = b
    for j in range(CONV_WIDTH):
        y = y + w[j] * hp[:, j:j + S]
    return y


def setup_inputs(seed: int = 0) -> dict:
    key = jax.random.key(seed)
    ks = jax.random.split(key, 24)
    f32 = jnp.float32
    nrm = lambda k, shape, scale: jax.random.normal(k, shape, f32) * scale
    L = DEPTH
    return {
        "x": jax.random.normal(ks[0], (BATCH, SEQ, D_MODEL), f32),
        "w_in": nrm(ks[1], (L, D_MODEL, IN_WIDTH), D_MODEL ** -0.5),
        "lambda_q1": nrm(ks[2], (L, DIFF_HEAD_DIM), 0.1),
        "lambda_k1": nrm(ks[3], (L, DIFF_HEAD_DIM), 0.1),
        "lambda_q2": nrm(ks[4], (L, DIFF_HEAD_DIM), 0.1),
        "lambda_k2": nrm(ks[5], (L, DIFF_HEAD_DIM), 0.1),
        "subln_g": 1.0 + nrm(ks[6], (L, 2 * DIFF_HEAD_DIM), 0.02),
        "gmlp_ln_g": 1.0 + nrm(ks[7], (L, GMLP_WIDTH), 0.02),
        "gmlp_ln_b": nrm(ks[8], (L, GMLP_WIDTH), 0.02),
        "w_spatial": nrm(ks[9], (L, N_GMLP_GROUPS, GMLP_CHUNK, GMLP_CHUNK), GMLP_CHUNK ** -0.5),
        "b_spatial": 1.0 + nrm(ks[10], (L, N_GMLP_GROUPS, GMLP_CHUNK), 0.02),
        "w_out": nrm(ks[11], (L, D_MODEL, D_MODEL), D_MODEL ** -0.5 * DEEPNORM_BETA),
        "ln1_g": 1.0 + nrm(ks[12], (L, D_MODEL), 0.02),
        "ln1_b": nrm(ks[13], (L, D_MODEL), 0.02),
        "w_gate": nrm(ks[14], (L, D_MODEL, D_FF), D_MODEL ** -0.5),
        "w_up": nrm(ks[15], (L, D_MODEL, D_FF), D_MODEL ** -0.5),
        "conv_w": nrm(ks[16], (L, CONV_WIDTH, D_FF), CONV_WIDTH ** -0.5),
        "conv_b": nrm(ks[17], (L, D_FF), 0.02),
        "w_down": nrm(ks[18], (L, D_FF, D_MODEL), D_FF ** -0.5 * DEEPNORM_BETA),
        "ln2_g": 1.0 + nrm(ks[19], (L, D_MODEL), 0.02),
        "ln2_b": nrm(ks[20], (L, D_MODEL), 0.02),
    }


def reference(x, w_in, lambda_q1, lambda_k1, lambda_q2, lambda_k2, subln_g, gmlp_ln_g, gmlp_ln_b,
              w_spatial, b_spatial, w_out, ln1_g, ln1_b, w_gate, w_up, conv_w, conv_b, w_down,
              ln2_g, ln2_b):
    B, S, _ = x.shape
    cos, sin = rope_tables(S)
    for l in range(DEPTH):
        lambda_init = 0.8 - 0.6 * math.exp(-0.3 * l)
        h = jnp.einsum('bsd,de->bse', x, w_in[l])
        qa = h[..., :ATT_WIDTH].reshape(B, S, 2 * N_DIFF_HEADS, DIFF_HEAD_DIM)
        ka = h[..., ATT_WIDTH:2 * ATT_WIDTH].reshape(B, S, 2 * N_DIFF_HEADS, DIFF_HEAD_DIM)
        va = h[..., 2 * ATT_WIDTH:3 * ATT_WIDTH].reshape(B, S, N_DIFF_HEADS, 2 * DIFF_HEAD_DIM)
        zb = h[..., 3 * ATT_WIDTH:]

        lam = (jnp.exp(jnp.sum(lambda_q1[l].astype(jnp.float32) * lambda_k1[l].astype(jnp.float32)))
               - jnp.exp(jnp.sum(lambda_q2[l].astype(jnp.float32) * lambda_k2[l].astype(jnp.float32)))
               + lambda_init)
        oa = diff_attention(qa, ka, va, lam, cos, sin)
        oa = (rms_norm(oa, subln_g[l]) * (1.0 - lambda_init)).reshape(B, S, ATT_WIDTH)

        ob = spatial_gating(jax.nn.gelu(zb, approximate=False), gmlp_ln_g[l], gmlp_ln_b[l],
                            w_spatial[l], b_spatial[l])

        mix = jnp.einsum('bse,ed->bsd', jnp.concatenate([oa, ob], axis=-1), w_out[l])
        x = layer_norm(DEEPNORM_ALPHA * x + mix, ln1_g[l], ln1_b[l])

        g = jnp.einsum('bsd,df->bsf', x, w_gate[l])
        up = jnp.einsum('bsd,df->bsf', x, w_up[l])
        g = causal_dwconv(g, conv_w[l], conv_b[l])
        f = jnp.einsum('bsf,fd->bsd', jax.nn.silu(g) * up, w_down[l])
        x = layer_norm(DEEPNORM_ALPHA * x + f, ln2_g[l], ln2_b[l])
    return x
```

```python
import functools
import math

import jax
import jax.numpy as jnp
from jax import lax
from jax.experimental import pallas as pl
from jax.experimental.pallas import tpu as pltpu

F32 = jnp.float32
BF16 = jnp.bfloat16

CHUNK = 64
N_DIFF_HEADS = 4
DIFF_HEAD_DIM = 64
PAIR_DIM = 2 * DIFF_HEAD_DIM
N_GMLP_GROUPS = 4
GMLP_GROUP_DIM = 128
GMLP_CHUNK = 128
CONV_WIDTH = 3
ROPE_THETA = 10000.0
LN_EPS = 1e-5

LANES = 128
SUBLANES = 8

IN_ROWS = 512
ATT_Q = 512
ATT_K = 512
FFN_ROWS = 512
FFN_COLS = 256
MASKED_SCORE = -1e30
VMEM_LIMIT = 56 * 1024 * 1024


def _layer_norm(z, g, b):
    mu = jnp.mean(z, axis=-1, keepdims=True)
    d = z - mu
    var = jnp.mean(d * d, axis=-1, keepdims=True)
    return d * lax.rsqrt(var + LN_EPS) * g + b


def _gelu(z):
    return 0.5 * z * (1.0 + lax.erf(z * (1.0 / math.sqrt(2.0))))


def _in_proj_kernel(x_ref, w_ref, cos_ref, sin_ref, lng_ref, lnb_ref, wsp_ref, bsp_ref,
                    q_ref, k_ref, v_ref, ob_ref, *, att_width, gmlp_width):
    rows = x_ref.shape[1]
    xb = x_ref[0].astype(BF16)

    def proj(col0, width):
        return jnp.dot(xb, w_ref[:, col0:col0 + width], preferred_element_type=F32)

    cos = cos_ref[...]
    sin = sin_ref[...]
    lane = lax.broadcasted_iota(jnp.int32, (rows, LANES), 1)
    first_half = (lane & (DIFF_HEAD_DIM // 2)) == 0
    low_head = lane < DIFF_HEAD_DIM

    def rope(t):
        rot = jnp.where(first_half, pltpu.roll(t, LANES - DIFF_HEAD_DIM // 2, 1),
                        pltpu.roll(t, DIFF_HEAD_DIM // 2, 1))
        return t * cos + rot * sin

    n_pairs = att_width // PAIR_DIM
    hq = proj(0, att_width)
    for p in range(n_pairs):
        r = rope(hq[:, p * PAIR_DIM:(p + 1) * PAIR_DIM]) * (DIFF_HEAD_DIM ** -0.5)
        q_ref[0, 2 * p] = jnp.where(low_head, r, 0.0).astype(BF16)
        q_ref[0, 2 * p + 1] = jnp.where(low_head, 0.0, r).astype(BF16)
    hk = proj(att_width, att_width)
    for p in range(n_pairs):
        k_ref[0, p] = rope(hk[:, p * PAIR_DIM:(p + 1) * PAIR_DIM]).astype(BF16)
    hv = proj(2 * att_width, att_width)
    for p in range(n_pairs):
        v_ref[0, p] = hv[:, p * PAIR_DIM:(p + 1) * PAIR_DIM].astype(BF16)

    u = _gelu(proj(3 * att_width, gmlp_width))
    vg = _gelu(proj(3 * att_width + gmlp_width, gmlp_width))
    n_chunks = rows // GMLP_CHUNK
    t_idx = lax.broadcasted_iota(jnp.int32, (GMLP_CHUNK, GMLP_CHUNK), 0)
    s_idx = lax.broadcasted_iota(jnp.int32, (GMLP_CHUNK, GMLP_CHUNK), 1)
    for g in range(N_GMLP_GROUPS):
        gs = slice(g * GMLP_GROUP_DIM, (g + 1) * GMLP_GROUP_DIM)
        vn = _layer_norm(vg[:, gs], lng_ref[:, gs], lnb_ref[:, gs]).astype(BF16)
        rhs = jnp.concatenate(
            [vn[c * GMLP_CHUNK:(c + 1) * GMLP_CHUNK] for c in range(n_chunks)], axis=1)
        w_causal = jnp.where(t_idx >= s_idx, wsp_ref[g], 0.0).astype(BF16)
        gate = jnp.dot(w_causal, rhs, preferred_element_type=F32)
        bias = bsp_ref[:, g:g + 1]
        for c in range(n_chunks):
            rs = slice(c * GMLP_CHUNK, (c + 1) * GMLP_CHUNK)
            gate_c = gate[:, c * GMLP_GROUP_DIM:(c + 1) * GMLP_GROUP_DIM] + bias
            ob_ref[0, rs, gs] = (u[rs, gs] * gate_c).astype(BF16)


def _attn_kernel(q_ref, k_ref, v_ref, lq1_ref, lk1_ref, lq2_ref, lk2_ref, g_ref,
                 o_ref, m_ref, l_ref, acc_ref, *, lambda_init):
    tq = q_ref.shape[2]
    tk = ATT_K
    i = pl.program_id(2)
    q = q_ref[0].reshape(2 * tq, PAIR_DIM)

    m_ref[...] = jnp.full(m_ref.shape, MASKED_SCORE, F32)
    l_ref[...] = jnp.zeros(l_ref.shape, F32)
    acc_ref[...] = jnp.zeros(acc_ref.shape, F32)

    def step(k_start, diag_offset):
        k = k_ref[0, 0, pl.ds(k_start, tk), :]
        v = v_ref[0, 0, pl.ds(k_start, tk), :]
        s = lax.dot_general(q, k, (((1,), (1,)), ((), ())), preferred_element_type=F32)
        if diag_offset is not None:
            q_chunk = (lax.broadcasted_iota(jnp.int32, s.shape, 0) & (tq - 1)) // CHUNK
            k_chunk = (lax.broadcasted_iota(jnp.int32, s.shape, 1) + diag_offset) // CHUNK
            s = jnp.where(k_chunk <= q_chunk, s, MASKED_SCORE)
        m_prev = m_ref[...]
        m_new = jnp.maximum(m_prev, jnp.max(s, axis=1, keepdims=True))
        alpha = jnp.exp(m_prev - m_new)
        p = jnp.exp(s - m_new)
        l_ref[...] = alpha * l_ref[...] + jnp.sum(p, axis=1, keepdims=True)
        acc_ref[...] = alpha * acc_ref[...] + jnp.dot(p.astype(BF16), v,
                                                      preferred_element_type=F32)
        m_ref[...] = m_new

    def full_step(j, carry):
        step(pl.multiple_of(j * tk, tk), None)
        return carry

    lax.fori_loop(0, i * (tq // tk), full_step, 0)
    for d in range(tq // tk):
        step(pl.multiple_of(i * tq + d * tk, tk), d * tk)

    lam = (jnp.exp(jnp.sum(lq1_ref[...] * lk1_ref[...], axis=1, keepdims=True))
           - jnp.exp(jnp.sum(lq2_ref[...] * lk2_ref[...], axis=1, keepdims=True))
           + lambda_init)
    o_all = acc_ref[...] / l_ref[...]
    o = o_all[:tq] - lam * o_all[tq:]
    o = o * lax.rsqrt(jnp.mean(o * o, axis=1, keepdims=True) + LN_EPS) * g_ref[...]
    o_ref[0] = (o * (1.0 - lambda_init)).astype(o_ref.dtype)


def _out_ffn_kernel(x_ref, oa_ref, ob_ref, wo_ref, g1_ref, b1_ref, wg_ref, wu_ref,
                    cw_ref, cb_ref, wd_ref, g2_ref, b2_ref, out_ref, tail_ref, *, alpha):
    rows = x_ref.shape[1]
    d_ff = wg_ref.shape[1]
    seq_tile = pl.program_id(1)

    cat = jnp.concatenate([oa_ref[0], ob_ref[0]], axis=1)
    mix = jnp.dot(cat, wo_ref[...], preferred_element_type=F32)
    y1 = _layer_norm(alpha * x_ref[0] + mix, g1_ref[...], b1_ref[...])
    yb = y1.astype(BF16)

    @pl.when(seq_tile == 0)
    def _():
        tail_ref[...] = jnp.zeros(tail_ref.shape, F32)

    f = jnp.zeros((rows, x_ref.shape[2]), F32)
    for c0 in range(0, d_ff, FFN_COLS):
        cs = slice(c0, c0 + FFN_COLS)
        g = jnp.dot(yb, wg_ref[:, cs], preferred_element_type=F32)
        up = jnp.dot(yb, wu_ref[:, cs], preferred_element_type=F32)
        prev = tail_ref[:, cs]
        tail_ref[:, cs] = g[rows - SUBLANES:]
        g_ext = jnp.concatenate([prev, g], axis=0)
        conv = cb_ref[:, cs] + cw_ref[2:3, cs] * g
        for tap in range(CONV_WIDTH - 1):
            back = CONV_WIDTH - 1 - tap
            conv = conv + cw_ref[tap:tap + 1, cs] * g_ext[SUBLANES - back:SUBLANES - back + rows]
        h = (conv * jax.nn.sigmoid(conv) * up).astype(BF16)
        f = f + jnp.dot(h, wd_ref[cs, :], preferred_element_type=F32)
    out_ref[0] = _layer_norm(alpha * y1 + f, g2_ref[...], b2_ref[...])


def _rope_tables(seq_len):
    pos = jnp.arange(seq_len, dtype=F32)
    inv_freq = 1.0 / (ROPE_THETA ** (jnp.arange(0, DIFF_HEAD_DIM, 2, dtype=F32) / DIFF_HEAD_DIM))
    ang = pos[:, None] * inv_freq[None, :]
    cos, sin = jnp.cos(ang), jnp.sin(ang)
    reps = LANES // (DIFF_HEAD_DIM // 2)
    cos_t = jnp.tile(cos, (1, reps))
    sin_t = jnp.tile(jnp.concatenate([-sin, sin], axis=1), (1, reps // 2))
    return cos_t, sin_t


def _const_spec(shape):
    return pl.BlockSpec(shape, lambda *_: (0,) * len(shape), pipeline_mode=pl.Buffered(1))


def _layer(x, cos_t, sin_t, lambda_init, alpha, w_in, lq1, lk1, lq2, lk2, subln_g, lng, lnb,
           w_sp, b_sp, w_out, ln1_g, ln1_b, w_gate, w_up, conv_w, conv_b, w_down, ln2_g, ln2_b):
    B, S, D = x.shape
    att_width = N_DIFF_HEADS * PAIR_DIM
    gmlp_width = N_GMLP_GROUPS * GMLP_GROUP_DIM
    d_ff = w_gate.shape[1]
    n_pairs = N_DIFF_HEADS
    row = lambda a: a.reshape(1, -1)

    q, k, v, ob = pl.pallas_call(
        functools.partial(_in_proj_kernel, att_width=att_width, gmlp_width=gmlp_width),
        grid=(B, S // IN_ROWS),
        in_specs=[
            pl.BlockSpec((1, IN_ROWS, D), lambda b, i: (b, i, 0)),
            _const_spec(w_in.shape),
            pl.BlockSpec((IN_ROWS, LANES), lambda b, i: (i, 0)),
            pl.BlockSpec((IN_ROWS, LANES), lambda b, i: (i, 0)),
            _const_spec((1, gmlp_width)),
            _const_spec((1, gmlp_width)),
            _const_spec(w_sp.shape),
            _const_spec((GMLP_CHUNK, N_GMLP_GROUPS)),
        ],
        out_specs=[
            pl.BlockSpec((1, 2 * n_pairs, IN_ROWS, PAIR_DIM), lambda b, i: (b, 0, i, 0)),
            pl.BlockSpec((1, n_pairs, IN_ROWS, PAIR_DIM), lambda b, i: (b, 0, i, 0)),
            pl.BlockSpec((1, n_pairs, IN_ROWS, PAIR_DIM), lambda b, i: (b, 0, i, 0)),
            pl.BlockSpec((1, IN_ROWS, gmlp_width), lambda b, i: (b, i, 0)),
        ],
        out_shape=[
            jax.ShapeDtypeStruct((B, 2 * n_pairs, S, PAIR_DIM), BF16),
            jax.ShapeDtypeStruct((B, n_pairs, S, PAIR_DIM), BF16),
            jax.ShapeDtypeStruct((B, n_pairs, S, PAIR_DIM), BF16),
            jax.ShapeDtypeStruct((B, S, gmlp_width), BF16),
        ],
        compiler_params=pltpu.CompilerParams(
            dimension_semantics=("arbitrary", "arbitrary"), vmem_limit_bytes=VMEM_LIMIT),
        name="in_proj_gmlp",
    )(x, w_in.astype(BF16), cos_t, sin_t, row(lng), row(lnb), w_sp, b_sp.T)

    oa = pl.pallas_call(
        functools.partial(_attn_kernel, lambda_init=lambda_init),
        grid=(B, n_pairs, S // ATT_Q),
        in_specs=[
            pl.BlockSpec((1, 2, ATT_Q, PAIR_DIM), lambda b, p, i: (b, p, i, 0)),
            pl.BlockSpec((1, 1, S, PAIR_DIM), lambda b, p, i: (b, p, 0, 0)),
            pl.BlockSpec((1, 1, S, PAIR_DIM), lambda b, p, i: (b, p, 0, 0)),
            _const_spec((1, DIFF_HEAD_DIM)),
            _const_spec((1, DIFF_HEAD_DIM)),
            _const_spec((1, DIFF_HEAD_DIM)),
            _const_spec((1, DIFF_HEAD_DIM)),
            _const_spec((1, PAIR_DIM)),
        ],
        out_specs=pl.BlockSpec((1, ATT_Q, PAIR_DIM), lambda b, p, i: (b, i, p)),
        out_shape=jax.ShapeDtypeStruct((B, S, att_width), BF16),
        scratch_shapes=[
            pltpu.VMEM((2 * ATT_Q, 1), F32),
            pltpu.VMEM((2 * ATT_Q, 1), F32),
            pltpu.VMEM((2 * ATT_Q, PAIR_DIM), F32),
        ],
        compiler_params=pltpu.CompilerParams(
            dimension_semantics=("arbitrary", "arbitrary", "arbitrary"),
            vmem_limit_bytes=VMEM_LIMIT),
        name="diff_attention",
    )(q, k, v, row(lq1), row(lk1), row(lq2), row(lk2), row(subln_g))

    out = pl.pallas_call(
        functools.partial(_out_ffn_kernel, alpha=alpha),
        grid=(B, S // FFN_ROWS),
        in_specs=[
            pl.BlockSpec((1, FFN_ROWS, D), lambda b, i: (b, i, 0)),
            pl.BlockSpec((1, FFN_ROWS, att_width), lambda b, i: (b, i, 0)),
            pl.BlockSpec((1, FFN_ROWS, gmlp_width), lambda b, i: (b, i, 0)),
            _const_spec(w_out.shape),
            _const_spec((1, D)),
            _const_spec((1, D)),
            _const_spec(w_gate.shape),
            _const_spec(w_up.shape),
            _const_spec(conv_w.shape),
            _const_spec((1, d_ff)),
            _const_spec(w_down.shape),
            _const_spec((1, D)),
            _const_spec((1, D)),
        ],
        out_specs=pl.BlockSpec((1, FFN_ROWS, D), lambda b, i: (b, i, 0)),
        out_shape=jax.ShapeDtypeStruct((B, S, D), x.dtype),
        scratch_shapes=[pltpu.VMEM((SUBLANES, d_ff), F32)],
        compiler_params=pltpu.CompilerParams(
            dimension_semantics=("arbitrary", "arbitrary"), vmem_limit_bytes=VMEM_LIMIT),
        name="out_proj_ffn",
    )(x, oa, ob, w_out.astype(BF16), row(ln1_g), row(ln1_b), w_gate.astype(BF16),
      w_up.astype(BF16), conv_w, row(conv_b), w_down.astype(BF16), row(ln2_g), row(ln2_b))
    return out


def kernel(x, w_in, lambda_q1, lambda_k1, lambda_q2, lambda_k2, subln_g, gmlp_ln_g, gmlp_ln_b,
           w_spatial, b_spatial, w_out, ln1_g, ln1_b, w_gate, w_up, conv_w, conv_b, w_down,
           ln2_g, ln2_b):
    depth = w_in.shape[0]
    alpha = (2 * depth) ** 0.25
    cos_t, sin_t = _rope_tables(x.shape[1])
    for l in range(depth):
        lambda_init = 0.8 - 0.6 * math.exp(-0.3 * l)
        x = _layer(x, cos_t, sin_t, lambda_init, alpha, w_in[l], lambda_q1[l], lambda_k1[l],
                   lambda_q2[l], lambda_k2[l], subln_g[l], gmlp_ln_g[l], gmlp_ln_b[l],
                   w_spatial[l], b_spatial[l], w_out[l], ln1_g[l], ln1_b[l], w_gate[l],
                   w_up[l], conv_w[l], conv_b[l], w_down[l], ln2_g[l], ln2_b[l])
    return x
```

```python
import functools
import math

import jax
import jax.numpy as jnp
from jax import lax
from jax.experimental import pallas as pl
from jax.experimental.pallas import tpu as pltpu

F32 = jnp.float32
BF16 = jnp.bfloat16

CHUNK = 64
N_DIFF_HEADS = 4
DIFF_HEAD_DIM = 64
PAIR_DIM = 2 * DIFF_HEAD_DIM
N_GMLP_GROUPS = 4
GMLP_GROUP_DIM = 128
GMLP_CHUNK = 128
CONV_WIDTH = 3
ROPE_THETA = 10000.0
LN_EPS = 1e-5

LANES = 128
SUBLANES = 8

IN_ROWS = 512
ATT_Q = 512
FFN_ROWS = 512
FFN_COLS = 256
MASKED_SCORE = -1e30
VMEM_LIMIT = 56 * 1024 * 1024


def _layer_norm(z, g, b):
    mu = jnp.mean(z, axis=-1, keepdims=True)
    d = z - mu
    var = jnp.mean(d * d, axis=-1, keepdims=True)
    return d * lax.rsqrt(var + LN_EPS) * g + b


def _gelu(z):
    return 0.5 * z * (1.0 + lax.erf(z * (1.0 / math.sqrt(2.0))))


def _in_proj_kernel(x_ref, w_ref, cos_ref, sin_ref, lng_ref, lnb_ref, wsp_ref, bsp_ref,
                    q_ref, k_ref, vt_ref, ob_ref, *, att_width, gmlp_width):
    rows = x_ref.shape[1]
    xb = x_ref[0].astype(BF16)

    def proj(col0, width):
        return jnp.dot(xb, w_ref[:, col0:col0 + width], preferred_element_type=F32)

    cos = cos_ref[...]
    sin = sin_ref[...]
    lane = lax.broadcasted_iota(jnp.int32, (rows, LANES), 1)
    first_half = (lane & (DIFF_HEAD_DIM // 2)) == 0
    low_head = lane < DIFF_HEAD_DIM

    def rope(t):
        rot = jnp.where(first_half, pltpu.roll(t, LANES - DIFF_HEAD_DIM // 2, 1),
                        pltpu.roll(t, DIFF_HEAD_DIM // 2, 1))
        return t * cos + rot * sin

    n_pairs = att_width // PAIR_DIM
    hq = proj(0, att_width)
    for p in range(n_pairs):
        r = rope(hq[:, p * PAIR_DIM:(p + 1) * PAIR_DIM]) * (DIFF_HEAD_DIM ** -0.5)
        q_ref[0, 2 * p] = jnp.where(low_head, r, 0.0).astype(BF16)
        q_ref[0, 2 * p + 1] = jnp.where(low_head, 0.0, r).astype(BF16)
    hk = proj(att_width, att_width)
    for p in range(n_pairs):
        k_ref[0, p] = rope(hk[:, p * PAIR_DIM:(p + 1) * PAIR_DIM]).astype(BF16)
    hv = proj(2 * att_width, att_width)
    for p in range(n_pairs):
        vt_ref[0, p, 0] = hv[:, p * PAIR_DIM:(p + 1) * PAIR_DIM].T.astype(BF16)

    u = _gelu(proj(3 * att_width, gmlp_width))
    vg = _gelu(proj(3 * att_width + gmlp_width, gmlp_width))
    n_chunks = rows // GMLP_CHUNK
    t_idx = lax.broadcasted_iota(jnp.int32, (GMLP_CHUNK, GMLP_CHUNK), 0)
    s_idx = lax.broadcasted_iota(jnp.int32, (GMLP_CHUNK, GMLP_CHUNK), 1)
    for g in range(N_GMLP_GROUPS):
        gs = slice(g * GMLP_GROUP_DIM, (g + 1) * GMLP_GROUP_DIM)
        vn = _layer_norm(vg[:, gs], lng_ref[:, gs], lnb_ref[:, gs]).astype(BF16)
        rhs = jnp.concatenate(
            [vn[c * GMLP_CHUNK:(c + 1) * GMLP_CHUNK] for c in range(n_chunks)], axis=1)
        w_causal = jnp.where(t_idx >= s_idx, wsp_ref[g], 0.0).astype(BF16)
        gate = jnp.dot(w_causal, rhs, preferred_element_type=F32)
        bias = bsp_ref[:, g:g + 1]
        for c in range(n_chunks):
            rs = slice(c * GMLP_CHUNK, (c + 1) * GMLP_CHUNK)
            gate_c = gate[:, c * GMLP_GROUP_DIM:(c + 1) * GMLP_GROUP_DIM] + bias
            ob_ref[0, rs, gs] = (u[rs, gs] * gate_c).astype(BF16)


def _attn_kernel(q_ref, k_ref, vt_ref, lq1_ref, lk1_ref, lq2_ref, lk2_ref, g_ref,
                 o_ref, m_ref, l_ref, acc_ref, *, lambda_init):
    tq = q_ref.shape[2]
    tk = vt_ref.shape[4]
    i = pl.program_id(2)
    q = q_ref[0].reshape(2 * tq, PAIR_DIM)

    m_ref[...] = jnp.full(m_ref.shape, MASKED_SCORE, F32)
    l_ref[...] = jnp.zeros(l_ref.shape, F32)
    acc_ref[...] = jnp.zeros(acc_ref.shape, F32)

    def step(kt, diag_offset):
        k = k_ref[0, 0, pl.ds(pl.multiple_of(kt * tk, tk), tk), :]
        vt = vt_ref[0, 0, kt]
        s = lax.dot_general(k, q, (((1,), (1,)), ((), ())),
                            preferred_element_type=F32)
        if diag_offset is not None:
            k_chunk = (lax.broadcasted_iota(jnp.int32, s.shape, 0) + diag_offset) // CHUNK
            q_chunk = (lax.broadcasted_iota(jnp.int32, s.shape, 1) & (tq - 1)) // CHUNK
            s = jnp.where(k_chunk <= q_chunk, s, MASKED_SCORE)
        m_prev = m_ref[...]
        m_new = jnp.maximum(m_prev, jnp.max(s, axis=0, keepdims=True))
        alpha = jnp.exp(m_prev - m_new)
        p = jnp.exp(s - m_new)
        l_ref[...] = alpha * l_ref[...] + jnp.sum(p, axis=0, keepdims=True)
        acc_ref[...] = alpha * acc_ref[...] + jnp.dot(vt, p.astype(BF16),
                                                      preferred_element_type=F32)
        m_ref[...] = m_new

    def full_step(kt, carry):
        step(kt, None)
        return carry

    lax.fori_loop(0, i * (tq // tk), full_step, 0)
    for d in range(tq // tk):
        step(i * (tq // tk) + d, d * tk)

    lam = (jnp.exp(jnp.sum(lq1_ref[...] * lk1_ref[...], axis=1, keepdims=True))
           - jnp.exp(jnp.sum(lq2_ref[...] * lk2_ref[...], axis=1, keepdims=True))
           + lambda_init)
    o_all = acc_ref[...] / l_ref[...]
    o = o_all[:, :tq] - lam * o_all[:, tq:]
    o = o * lax.rsqrt(jnp.mean(o * o, axis=0, keepdims=True) + LN_EPS)
    o = o.T * (g_ref[...] * (1.0 - lambda_init))
    o_ref[0] = o.astype(o_ref.dtype)


def _out_ffn_kernel(x_ref, oa_ref, ob_ref, wo_ref, g1_ref, b1_ref, wg_ref, wu_ref,
                    cw_ref, cb_ref, wd_ref, g2_ref, b2_ref, out_ref, tail_ref, *, alpha):
    rows = x_ref.shape[1]
    d_ff = wg_ref.shape[1]
    seq_tile = pl.program_id(1)

    cat = jnp.concatenate([oa_ref[0], ob_ref[0]], axis=1)
    mix = jnp.dot(cat, wo_ref[...], preferred_element_type=F32)
    y1 = _layer_norm(alpha * x_ref[0] + mix, g1_ref[...], b1_ref[...])
    yb = y1.astype(BF16)

    @pl.when(seq_tile == 0)
    def _():
        tail_ref[...] = jnp.zeros(tail_ref.shape, F32)

    f = jnp.zeros((rows, x_ref.shape[2]), F32)
    for c0 in range(0, d_ff, FFN_COLS):
        cs = slice(c0, c0 + FFN_COLS)
        g = jnp.dot(yb, wg_ref[:, cs], preferred_element_type=F32)
        up = jnp.dot(yb, wu_ref[:, cs], preferred_element_type=F32)
        prev = tail_ref[:, cs]
        tail_ref[:, cs] = g[rows - SUBLANES:]
        g_ext = jnp.concatenate([prev, g], axis=0)
        conv = cb_ref[:, cs] + cw_ref[2:3, cs] * g
        for tap in range(CONV_WIDTH - 1):
            back = CONV_WIDTH - 1 - tap
            conv = conv + cw_ref[tap:tap + 1, cs] * g_ext[SUBLANES - back:SUBLANES - back + rows]
        h = (conv * jax.nn.sigmoid(conv) * up).astype(BF16)
        f = f + jnp.dot(h, wd_ref[cs, :], preferred_element_type=F32)
    out_ref[0] = _layer_norm(alpha * y1 + f, g2_ref[...], b2_ref[...])


def _rope_tables(seq_len):
    pos = jnp.arange(seq_len, dtype=F32)
    inv_freq = 1.0 / (ROPE_THETA ** (jnp.arange(0, DIFF_HEAD_DIM, 2, dtype=F32) / DIFF_HEAD_DIM))
    ang = pos[:, None] * inv_freq[None, :]
    cos, sin = jnp.cos(ang), jnp.sin(ang)
    reps = LANES // (DIFF_HEAD_DIM // 2)
    cos_t = jnp.tile(cos, (1, reps))
    sin_t = jnp.tile(jnp.concatenate([-sin, sin], axis=1), (1, reps // 2))
    return cos_t, sin_t


def _const_spec(shape):
    return pl.BlockSpec(shape, lambda *_: (0,) * len(shape), pipeline_mode=pl.Buffered(1))


def _layer(x, cos_t, sin_t, lambda_init, alpha, w_in, lq1, lk1, lq2, lk2, subln_g, lng, lnb,
           w_sp, b_sp, w_out, ln1_g, ln1_b, w_gate, w_up, conv_w, conv_b, w_down, ln2_g, ln2_b):
    B, S, D = x.shape
    att_width = N_DIFF_HEADS * PAIR_DIM
    gmlp_width = N_GMLP_GROUPS * GMLP_GROUP_DIM
    d_ff = w_gate.shape[1]
    n_pairs = N_DIFF_HEADS
    row = lambda a: a.reshape(1, -1)

    q, k, v, ob = pl.pallas_call(
        functools.partial(_in_proj_kernel, att_width=att_width, gmlp_width=gmlp_width),
        grid=(B, S // IN_ROWS),
        in_specs=[
            pl.BlockSpec((1, IN_ROWS, D), lambda b, i: (b, i, 0)),
            _const_spec(w_in.shape),
            pl.BlockSpec((IN_ROWS, LANES), lambda b, i: (i, 0)),
            pl.BlockSpec((IN_ROWS, LANES), lambda b, i: (i, 0)),
            _const_spec((1, gmlp_width)),
            _const_spec((1, gmlp_width)),
            _const_spec(w_sp.shape),
            _const_spec((GMLP_CHUNK, N_GMLP_GROUPS)),
        ],
        out_specs=[
            pl.BlockSpec((1, 2 * n_pairs, IN_ROWS, PAIR_DIM), lambda b, i: (b, 0, i, 0)),
            pl.BlockSpec((1, n_pairs, IN_ROWS, PAIR_DIM), lambda b, i: (b, 0, i, 0)),
            pl.BlockSpec((1, n_pairs, 1, PAIR_DIM, IN_ROWS), lambda b, i: (b, 0, i, 0, 0)),
            pl.BlockSpec((1, IN_ROWS, gmlp_width), lambda b, i: (b, i, 0)),
        ],
        out_shape=[
            jax.ShapeDtypeStruct((B, 2 * n_pairs, S, PAIR_DIM), BF16),
            jax.ShapeDtypeStruct((B, n_pairs, S, PAIR_DIM), BF16),
            jax.ShapeDtypeStruct((B, n_pairs, S // IN_ROWS, PAIR_DIM, IN_ROWS), BF16),
            jax.ShapeDtypeStruct((B, S, gmlp_width), BF16),
        ],
        compiler_params=pltpu.CompilerParams(
            dimension_semantics=("arbitrary", "arbitrary"), vmem_limit_bytes=VMEM_LIMIT),
        name="in_proj_gmlp",
    )(x, w_in.astype(BF16), cos_t, sin_t, row(lng), row(lnb), w_sp, b_sp.T)

    oa = pl.pallas_call(
        functools.partial(_attn_kernel, lambda_init=lambda_init),
        grid=(B, n_pairs, S // ATT_Q),
        in_specs=[
            pl.BlockSpec((1, 2, ATT_Q, PAIR_DIM), lambda b, p, i: (b, p, i, 0)),
            pl.BlockSpec((1, 1, S, PAIR_DIM), lambda b, p, i: (b, p, 0, 0)),
            pl.BlockSpec((1, 1, S // IN_ROWS, PAIR_DIM, IN_ROWS), lambda b, p, i: (b, p, 0, 0, 0)),
            _const_spec((1, DIFF_HEAD_DIM)),
            _const_spec((1, DIFF_HEAD_DIM)),
            _const_spec((1, DIFF_HEAD_DIM)),
            _const_spec((1, DIFF_HEAD_DIM)),
            _const_spec((1, PAIR_DIM)),
        ],
        out_specs=pl.BlockSpec((1, ATT_Q, PAIR_DIM), lambda b, p, i: (b, i, p)),
        out_shape=jax.ShapeDtypeStruct((B, S, att_width), BF16),
        scratch_shapes=[
            pltpu.VMEM((1, 2 * ATT_Q), F32),
            pltpu.VMEM((1, 2 * ATT_Q), F32),
            pltpu.VMEM((PAIR_DIM, 2 * ATT_Q), F32),
        ],
        compiler_params=pltpu.CompilerParams(
            dimension_semantics=("arbitrary", "arbitrary", "arbitrary"),
            vmem_limit_bytes=VMEM_LIMIT),
        name="diff_attention",
    )(q, k, v, row(lq1), row(lk1), row(lq2), row(lk2), row(subln_g))

    out = pl.pallas_call(
        functools.partial(_out_ffn_kernel, alpha=alpha),
        grid=(B, S // FFN_ROWS),
        in_specs=[
            pl.BlockSpec((1, FFN_ROWS, D), lambda b, i: (b, i, 0)),
            pl.BlockSpec((1, FFN_ROWS, att_width), lambda b, i: (b, i, 0)),
            pl.BlockSpec((1, FFN_ROWS, gmlp_width), lambda b, i: (b, i, 0)),
            _const_spec(w_out.shape),
            _const_spec((1, D)),
            _const_spec((1, D)),
            _const_spec(w_gate.shape),
            _const_spec(w_up.shape),
            _const_spec(conv_w.shape),
            _const_spec((1, d_ff)),
            _const_spec(w_down.shape),
            _const_spec((1, D)),
            _const_spec((1, D)),
        ],
        out_specs=pl.BlockSpec((1, FFN_ROWS, D), lambda b, i: (b, i, 0)),
        out_shape=jax.ShapeDtypeStruct((B, S, D), x.dtype),
        scratch_shapes=[pltpu.VMEM((SUBLANES, d_ff), F32)],
        compiler_params=pltpu.CompilerParams(
            dimension_semantics=("arbitrary", "arbitrary"), vmem_limit_bytes=VMEM_LIMIT),
        name="out_proj_ffn",
    )(x, oa, ob, w_out.astype(BF16), row(ln1_g), row(ln1_b), w_gate.astype(BF16),
      w_up.astype(BF16), conv_w, row(conv_b), w_down.astype(BF16), row(ln2_g), row(ln2_b))
    return out


def kernel(x, w_in, lambda_q1, lambda_k1, lambda_q2, lambda_k2, subln_g, gmlp_ln_g, gmlp_ln_b,
           w_spatial, b_spatial, w_out, ln1_g, ln1_b, w_gate, w_up, conv_w, conv_b, w_down,
           ln2_g, ln2_b):
    depth = w_in.shape[0]
    alpha = (2 * depth) ** 0.25
    cos_t, sin_t = _rope_tables(x.shape[1])
    for l in range(depth):
        lambda_init = 0.8 - 0.6 * math.exp(-0.3 * l)
        x = _layer(x, cos_t, sin_t, lambda_init, alpha, w_in[l], lambda_q1[l], lambda_k1[l],
                   lambda_q2[l], lambda_k2[l], subln_g[l], gmlp_ln_g[l], gmlp_ln_b[l],
                   w_spatial[l], b_spatial[l], w_out[l], ln1_g[l], ln1_b[l], w_gate[l],
                   w_up[l], conv_w[l], conv_b[l], w_down[l], ln2_g[l], ln2_b[l])
    return x
```

```python
import functools
import math

import jax
import jax.numpy as jnp
from jax import lax
from jax.experimental import pallas as pl
from jax.experimental.pallas import tpu as pltpu

F32 = jnp.float32
BF16 = jnp.bfloat16

CHUNK = 64
N_DIFF_HEADS = 4
DIFF_HEAD_DIM = 64
PAIR_DIM = 2 * DIFF_HEAD_DIM
N_GMLP_GROUPS = 4
GMLP_GROUP_DIM = 128
GMLP_CHUNK = 128
CONV_WIDTH = 3
ROPE_THETA = 10000.0
LN_EPS = 1e-5

LANES = 128
SUBLANES = 8
BF16_SUBLANES = 16
VT_ROWS = PAIR_DIM + BF16_SUBLANES
LOG2_E = math.log2(math.e)

IN_ROWS = 512
ATT_Q = 512
ATT_COLS = 256
FFN_ROWS = 512
FFN_COLS = 256
MASKED_SCORE = -1e30
VMEM_LIMIT = 56 * 1024 * 1024


def _layer_norm(z, g, b):
    mu = jnp.mean(z, axis=-1, keepdims=True)
    d = z - mu
    var = jnp.mean(d * d, axis=-1, keepdims=True)
    return d * lax.rsqrt(var + LN_EPS) * g + b


def _gelu(z):
    return 0.5 * z * (1.0 + lax.erf(z * (1.0 / math.sqrt(2.0))))


def _in_proj_kernel(x_ref, w_ref, cos_ref, sin_ref, lng_ref, lnb_ref, wsp_ref, bsp_ref,
                    q_ref, k_ref, vt_ref, ob_ref, *, att_width, gmlp_width):
    rows = x_ref.shape[1]
    xb = x_ref[0].astype(BF16)

    def proj(col0, width):
        return jnp.dot(xb, w_ref[:, col0:col0 + width], preferred_element_type=F32)

    cos = cos_ref[...]
    sin = sin_ref[...]
    lane = lax.broadcasted_iota(jnp.int32, (rows, LANES), 1)
    first_half = (lane & (DIFF_HEAD_DIM // 2)) == 0
    low_head = lane < DIFF_HEAD_DIM

    def rope(t):
        rot = jnp.where(first_half, pltpu.roll(t, LANES - DIFF_HEAD_DIM // 2, 1),
                        pltpu.roll(t, DIFF_HEAD_DIM // 2, 1))
        return t * cos + rot * sin

    n_pairs = att_width // PAIR_DIM
    hq = proj(0, att_width)
    for p in range(n_pairs):
        r = rope(hq[:, p * PAIR_DIM:(p + 1) * PAIR_DIM]) * (DIFF_HEAD_DIM ** -0.5 * LOG2_E)
        q_ref[0, 2 * p] = jnp.where(low_head, r, 0.0).astype(BF16)
        q_ref[0, 2 * p + 1] = jnp.where(low_head, 0.0, r).astype(BF16)
    hk = proj(att_width, att_width)
    for p in range(n_pairs):
        k_ref[0, p] = rope(hk[:, p * PAIR_DIM:(p + 1) * PAIR_DIM]).astype(BF16)
    hv = proj(2 * att_width, att_width)
    for p in range(n_pairs):
        vt_ref[0, p, 0, :PAIR_DIM, :] = hv[:, p * PAIR_DIM:(p + 1) * PAIR_DIM].T.astype(BF16)
        extra = lax.broadcasted_iota(jnp.int32, (BF16_SUBLANES, rows), 0)
        vt_ref[0, p, 0, PAIR_DIM:, :] = jnp.where(extra == 0, 1.0, 0.0).astype(BF16)

    u = _gelu(proj(3 * att_width, gmlp_width))
    vg = _gelu(proj(3 * att_width + gmlp_width, gmlp_width))
    n_chunks = rows // GMLP_CHUNK
    t_idx = lax.broadcasted_iota(jnp.int32, (GMLP_CHUNK, GMLP_CHUNK), 0)
    s_idx = lax.broadcasted_iota(jnp.int32, (GMLP_CHUNK, GMLP_CHUNK), 1)
    for g in range(N_GMLP_GROUPS):
        gs = slice(g * GMLP_GROUP_DIM, (g + 1) * GMLP_GROUP_DIM)
        vn = _layer_norm(vg[:, gs], lng_ref[:, gs], lnb_ref[:, gs]).astype(BF16)
        rhs = jnp.concatenate(
            [vn[c * GMLP_CHUNK:(c + 1) * GMLP_CHUNK] for c in range(n_chunks)], axis=1)
        w_causal = jnp.where(t_idx >= s_idx, wsp_ref[g], 0.0).astype(BF16)
        gate = jnp.dot(w_causal, rhs, preferred_element_type=F32)
        bias = bsp_ref[:, g:g + 1]
        for c in range(n_chunks):
            rs = slice(c * GMLP_CHUNK, (c + 1) * GMLP_CHUNK)
            gate_c = gate[:, c * GMLP_GROUP_DIM:(c + 1) * GMLP_GROUP_DIM] + bias
            ob_ref[0, rs, gs] = (u[rs, gs] * gate_c).astype(BF16)


def _attn_kernel(q_ref, k_ref, vt_ref, lq1_ref, lk1_ref, lq2_ref, lk2_ref, g_ref,
                 o_ref, m_ref, acc_ref, sa_ref, sb_ref, ma_ref, mb_ref, *, lambda_init):
    tq = q_ref.shape[2]
    tk = vt_ref.shape[4]
    i = pl.program_id(2)
    q = q_ref[0].reshape(2 * tq, PAIR_DIM)
    groups = range(0, 2 * tq, ATT_COLS)

    m_ref[...] = jnp.full(m_ref.shape, MASKED_SCORE, F32)
    acc_ref[...] = jnp.zeros(acc_ref.shape, F32)

    def scores(kt, s_ref, mx_ref, diagonal):
        k = k_ref[0, 0, pl.ds(pl.multiple_of(kt * tk, tk), tk), :]
        for c0 in groups:
            cols = slice(c0, c0 + ATT_COLS)
            s = lax.dot_general(k, q[cols], (((1,), (1,)), ((), ())),
                                preferred_element_type=F32)
            if diagonal:
                k_chunk = lax.broadcasted_iota(jnp.int32, s.shape, 0) // CHUNK
                q_pos = lax.broadcasted_iota(jnp.int32, s.shape, 1) + (c0 & (tq - 1))
                s = jnp.where(k_chunk <= q_pos // CHUNK, s, MASKED_SCORE)
            s_ref[:, cols] = s
            mx_ref[:, cols] = jnp.max(s, axis=0, keepdims=True)

    def update(kt, s_ref, mx_ref):
        vt = vt_ref[0, 0, kt]
        for c0 in groups:
            cols = slice(c0, c0 + ATT_COLS)
            m_prev = m_ref[:, cols]
            m_new = jnp.maximum(m_prev, mx_ref[:, cols])
            alpha = jnp.exp2(m_prev - m_new)
            p = jnp.exp2(s_ref[:, cols] - m_new)
            acc_ref[:, cols] = alpha * acc_ref[:, cols] + jnp.dot(
                vt, p.astype(BF16), preferred_element_type=F32)
            m_ref[:, cols] = m_new

    scores(i, sa_ref, ma_ref, True)

    def two_tiles(t, carry):
        scores(2 * t, sb_ref, mb_ref, False)
        update(jnp.where(t == 0, i, 2 * t - 1), sa_ref, ma_ref)
        scores(2 * t + 1, sa_ref, ma_ref, False)
        update(2 * t, sb_ref, mb_ref)
        return carry

    lax.fori_loop(0, i // 2, two_tiles, 0)

    @pl.when(i % 2 == 0)
    def _():
        update(jnp.where(i == 0, i, i - 1), sa_ref, ma_ref)

    @pl.when(i % 2 == 1)
    def _():
        scores(i - 1, sb_ref, mb_ref, False)
        update(jnp.where(i == 1, i, i - 2), sa_ref, ma_ref)
        update(i - 1, sb_ref, mb_ref)

    lam = (jnp.exp(jnp.sum(lq1_ref[...] * lk1_ref[...], axis=1, keepdims=True))
           - jnp.exp(jnp.sum(lq2_ref[...] * lk2_ref[...], axis=1, keepdims=True))
           + lambda_init)
    o_all = acc_ref[:PAIR_DIM, :] / acc_ref[PAIR_DIM:PAIR_DIM + 1, :]
    o = o_all[:, :tq] - lam * o_all[:, tq:]
    o = o * lax.rsqrt(jnp.mean(o * o, axis=0, keepdims=True) + LN_EPS)
    o = o.T * (g_ref[...] * (1.0 - lambda_init))
    o_ref[0] = o.astype(o_ref.dtype)


def _out_ffn_kernel(x_ref, oa_ref, ob_ref, wo_ref, g1_ref, b1_ref, wg_ref, wu_ref,
                    cw_ref, cb_ref, wd_ref, g2_ref, b2_ref, out_ref, tail_ref, *, alpha):
    rows = x_ref.shape[1]
    d_ff = wg_ref.shape[1]
    seq_tile = pl.program_id(1)

    cat = jnp.concatenate([oa_ref[0], ob_ref[0]], axis=1)
    mix = jnp.dot(cat, wo_ref[...], preferred_element_type=F32)
    y1 = _layer_norm(alpha * x_ref[0] + mix, g1_ref[...], b1_ref[...])
    yb = y1.astype(BF16)

    @pl.when(seq_tile == 0)
    def _():
        tail_ref[...] = jnp.zeros(tail_ref.shape, F32)

    f = jnp.zeros((rows, x_ref.shape[2]), F32)
    for c0 in range(0, d_ff, FFN_COLS):
        cs = slice(c0, c0 + FFN_COLS)
        g = jnp.dot(yb, wg_ref[:, cs], preferred_element_type=F32)
        up = jnp.dot(yb, wu_ref[:, cs], preferred_element_type=F32)
        prev = tail_ref[:, cs]
        tail_ref[:, cs] = g[rows - SUBLANES:]
        g_ext = jnp.concatenate([prev, g], axis=0)
        conv = cb_ref[:, cs] + cw_ref[2:3, cs] * g
        for tap in range(CONV_WIDTH - 1):
            back = CONV_WIDTH - 1 - tap
            conv = conv + cw_ref[tap:tap + 1, cs] * g_ext[SUBLANES - back:SUBLANES - back + rows]
        h = (conv * jax.nn.sigmoid(conv) * up).astype(BF16)
        f = f + jnp.dot(h, wd_ref[cs, :], preferred_element_type=F32)
    out_ref[0] = _layer_norm(alpha * y1 + f, g2_ref[...], b2_ref[...])


def _rope_tables(seq_len):
    pos = jnp.arange(seq_len, dtype=F32)
    inv_freq = 1.0 / (ROPE_THETA ** (jnp.arange(0, DIFF_HEAD_DIM, 2, dtype=F32) / DIFF_HEAD_DIM))
    ang = pos[:, None] * inv_freq[None, :]
    cos, sin = jnp.cos(ang), jnp.sin(ang)
    reps = LANES // (DIFF_HEAD_DIM // 2)
    cos_t = jnp.tile(cos, (1, reps))
    sin_t = jnp.tile(jnp.concatenate([-sin, sin], axis=1), (1, reps // 2))
    return cos_t, sin_t


def _const_spec(shape):
    return pl.BlockSpec(shape, lambda *_: (0,) * len(shape), pipeline_mode=pl.Buffered(1))


def _layer(x, cos_t, sin_t, lambda_init, alpha, w_in, lq1, lk1, lq2, lk2, subln_g, lng, lnb,
           w_sp, b_sp, w_out, ln1_g, ln1_b, w_gate, w_up, conv_w, conv_b, w_down, ln2_g, ln2_b):
    B, S, D = x.shape
    att_width = N_DIFF_HEADS * PAIR_DIM
    gmlp_width = N_GMLP_GROUPS * GMLP_GROUP_DIM
    d_ff = w_gate.shape[1]
    n_pairs = N_DIFF_HEADS
    row = lambda a: a.reshape(1, -1)

    q, k, v, ob = pl.pallas_call(
        functools.partial(_in_proj_kernel, att_width=att_width, gmlp_width=gmlp_width),
        grid=(B, S // IN_ROWS),
        in_specs=[
            pl.BlockSpec((1, IN_ROWS, D), lambda b, i: (b, i, 0)),
            _const_spec(w_in.shape),
            pl.BlockSpec((IN_ROWS, LANES), lambda b, i: (i, 0)),
            pl.BlockSpec((IN_ROWS, LANES), lambda b, i: (i, 0)),
            _const_spec((1, gmlp_width)),
            _const_spec((1, gmlp_width)),
            _const_spec(w_sp.shape),
            _const_spec((GMLP_CHUNK, N_GMLP_GROUPS)),
        ],
        out_specs=[
            pl.BlockSpec((1, 2 * n_pairs, IN_ROWS, PAIR_DIM), lambda b, i: (b, 0, i, 0)),
            pl.BlockSpec((1, n_pairs, IN_ROWS, PAIR_DIM), lambda b, i: (b, 0, i, 0)),
            pl.BlockSpec((1, n_pairs, 1, VT_ROWS, IN_ROWS), lambda b, i: (b, 0, i, 0, 0)),
            pl.BlockSpec((1, IN_ROWS, gmlp_width), lambda b, i: (b, i, 0)),
        ],
        out_shape=[
            jax.ShapeDtypeStruct((B, 2 * n_pairs, S, PAIR_DIM), BF16),
            jax.ShapeDtypeStruct((B, n_pairs, S, PAIR_DIM), BF16),
            jax.ShapeDtypeStruct((B, n_pairs, S // IN_ROWS, VT_ROWS, IN_ROWS), BF16),
            jax.ShapeDtypeStruct((B, S, gmlp_width), BF16),
        ],
        compiler_params=pltpu.CompilerParams(
            dimension_semantics=("arbitrary", "arbitrary"), vmem_limit_bytes=VMEM_LIMIT),
        name="in_proj_gmlp",
    )(x, w_in.astype(BF16), cos_t, sin_t, row(lng), row(lnb), w_sp, b_sp.T)

    oa = pl.pallas_call(
        functools.partial(_attn_kernel, lambda_init=lambda_init),
        grid=(B, n_pairs, S // ATT_Q),
        in_specs=[
            pl.BlockSpec((1, 2, ATT_Q, PAIR_DIM), lambda b, p, i: (b, p, i, 0)),
            pl.BlockSpec((1, 1, S, PAIR_DIM), lambda b, p, i: (b, p, 0, 0)),
            pl.BlockSpec((1, 1, S // IN_ROWS, VT_ROWS, IN_ROWS), lambda b, p, i: (b, p, 0, 0, 0)),
            _const_spec((1, DIFF_HEAD_DIM)),
            _const_spec((1, DIFF_HEAD_DIM)),
            _const_spec((1, DIFF_HEAD_DIM)),
            _const_spec((1, DIFF_HEAD_DIM)),
            _const_spec((1, PAIR_DIM)),
        ],
        out_specs=pl.BlockSpec((1, ATT_Q, PAIR_DIM), lambda b, p, i: (b, i, p)),
        out_shape=jax.ShapeDtypeStruct((B, S, att_width), BF16),
        scratch_shapes=[
            pltpu.VMEM((1, 2 * ATT_Q), F32),
            pltpu.VMEM((VT_ROWS, 2 * ATT_Q), F32),
            pltpu.VMEM((IN_ROWS, 2 * ATT_Q), F32),
            pltpu.VMEM((IN_ROWS, 2 * ATT_Q), F32),
            pltpu.VMEM((1, 2 * ATT_Q), F32),
            pltpu.VMEM((1, 2 * ATT_Q), F32),
        ],
        compiler_params=pltpu.CompilerParams(
            dimension_semantics=("arbitrary", "arbitrary", "arbitrary"),
            vmem_limit_bytes=VMEM_LIMIT),
        name="diff_attention",
    )(q, k, v, row(lq1), row(lk1), row(lq2), row(lk2), row(subln_g))

    out = pl.pallas_call(
        functools.partial(_out_ffn_kernel, alpha=alpha),
        grid=(B, S // FFN_ROWS),
        in_specs=[
            pl.BlockSpec((1, FFN_ROWS, D), lambda b, i: (b, i, 0)),
            pl.BlockSpec((1, FFN_ROWS, att_width), lambda b, i: (b, i, 0)),
            pl.BlockSpec((1, FFN_ROWS, gmlp_width), lambda b, i: (b, i, 0)),
            _const_spec(w_out.shape),
            _const_spec((1, D)),
            _const_spec((1, D)),
            _const_spec(w_gate.shape),
            _const_spec(w_up.shape),
            _const_spec(conv_w.shape),
            _const_spec((1, d_ff)),
            _const_spec(w_down.shape),
            _const_spec((1, D)),
            _const_spec((1, D)),
        ],
        out_specs=pl.BlockSpec((1, FFN_ROWS, D), lambda b, i: (b, i, 0)),
        out_shape=jax.ShapeDtypeStruct((B, S, D), x.dtype),
        scratch_shapes=[pltpu.VMEM((SUBLANES, d_ff), F32)],
        compiler_params=pltpu.CompilerParams(
            dimension_semantics=("arbitrary", "arbitrary"), vmem_limit_bytes=VMEM_LIMIT),
        name="out_proj_ffn",
    )(x, oa, ob, w_out.astype(BF16), row(ln1_g), row(ln1_b), w_gate.astype(BF16),
      w_up.astype(BF16), conv_w, row(conv_b), w_down.astype(BF16), row(ln2_g), row(ln2_b))
    return out


def kernel(x, w_in, lambda_q1, lambda_k1, lambda_q2, lambda_k2, subln_g, gmlp_ln_g, gmlp_ln_b,
           w_spatial, b_spatial, w_out, ln1_g, ln1_b, w_gate, w_up, conv_w, conv_b, w_down,
           ln2_g, ln2_b):
    depth = w_in.shape[0]
    alpha = (2 * depth) ** 0.25
    cos_t, sin_t = _rope_tables(x.shape[1])
    for l in range(depth):
        lambda_init = 0.8 - 0.6 * math.exp(-0.3 * l)
        x = _layer(x, cos_t, sin_t, lambda_init, alpha, w_in[l], lambda_q1[l], lambda_k1[l],
                   lambda_q2[l], lambda_k2[l], subln_g[l], gmlp_ln_g[l], gmlp_ln_b[l],
                   w_spatial[l], b_spatial[l], w_out[l], ln1_g[l], ln1_b[l], w_gate[l],
                   w_up[l], conv_w[l], conv_b[l], w_down[l], ln2_g[l], ln2_b[l])
    return x
```

```python
import functools
import math

import jax
import jax.numpy as jnp
from jax import lax
from jax.experimental import pallas as pl
from jax.experimental.pallas import tpu as pltpu

F32 = jnp.float32
BF16 = jnp.bfloat16

CHUNK = 64
N_DIFF_HEADS = 4
DIFF_HEAD_DIM = 64
PAIR_DIM = 2 * DIFF_HEAD_DIM
N_GMLP_GROUPS = 4
GMLP_GROUP_DIM = 128
GMLP_CHUNK = 128
CONV_WIDTH = 3
ROPE_THETA = 10000.0
LN_EPS = 1e-5

LANES = 128
SUBLANES = 8
BF16_SUBLANES = 16
VT_ROWS = PAIR_DIM + BF16_SUBLANES
LOG2_E = math.log2(math.e)

IN_ROWS = 512
ATT_COLS = 256
FFN_ROWS = 512
FFN_COLS = 768
MASKED_SCORE = -1e30
VMEM_LIMIT = 56 * 1024 * 1024


def _layer_norm(z, g, b):
    mu = jnp.mean(z, axis=-1, keepdims=True)
    d = z - mu
    var = jnp.mean(d * d, axis=-1, keepdims=True)
    return d * lax.rsqrt(var + LN_EPS) * g + b


def _gelu(z):
    return 0.5 * z * (1.0 + lax.erf(z * (1.0 / math.sqrt(2.0))))


def _in_proj_kernel(x_ref, w_ref, cos_ref, sin_ref, lng_ref, lnb_ref, wsp_ref, bsp_ref,
                    q_ref, k_ref, vt_ref, ob_ref, *, att_width, gmlp_width):
    rows = x_ref.shape[1]
    xb = x_ref[0].astype(BF16)

    def proj(col0, width):
        return jnp.dot(xb, w_ref[:, col0:col0 + width], preferred_element_type=F32)

    cos = cos_ref[...]
    sin = sin_ref[...]
    lane = lax.broadcasted_iota(jnp.int32, (rows, LANES), 1)
    first_half = (lane & (DIFF_HEAD_DIM // 2)) == 0
    low_head = lane < DIFF_HEAD_DIM

    def rope(t):
        rot = jnp.where(first_half, pltpu.roll(t, LANES - DIFF_HEAD_DIM // 2, 1),
                        pltpu.roll(t, DIFF_HEAD_DIM // 2, 1))
        return t * cos + rot * sin

    n_pairs = att_width // PAIR_DIM
    hq = proj(0, att_width)
    for p in range(n_pairs):
        r = rope(hq[:, p * PAIR_DIM:(p + 1) * PAIR_DIM]) * (DIFF_HEAD_DIM ** -0.5 * LOG2_E)
        q_ref[0, 2 * p] = jnp.where(low_head, r, 0.0).astype(BF16)
        q_ref[0, 2 * p + 1] = jnp.where(low_head, 0.0, r).astype(BF16)
    hk = proj(att_width, att_width)
    for p in range(n_pairs):
        k_ref[0, p] = rope(hk[:, p * PAIR_DIM:(p + 1) * PAIR_DIM]).astype(BF16)
    hv = proj(2 * att_width, att_width)
    for p in range(n_pairs):
        vt_ref[0, p, 0, :PAIR_DIM, :] = hv[:, p * PAIR_DIM:(p + 1) * PAIR_DIM].T.astype(BF16)
        extra = lax.broadcasted_iota(jnp.int32, (BF16_SUBLANES, rows), 0)
        vt_ref[0, p, 0, PAIR_DIM:, :] = jnp.where(extra == 0, 1.0, 0.0).astype(BF16)

    u = _gelu(proj(3 * att_width, gmlp_width))
    vg = _gelu(proj(3 * att_width + gmlp_width, gmlp_width))
    n_chunks = rows // GMLP_CHUNK
    t_idx = lax.broadcasted_iota(jnp.int32, (GMLP_CHUNK, GMLP_CHUNK), 0)
    s_idx = lax.broadcasted_iota(jnp.int32, (GMLP_CHUNK, GMLP_CHUNK), 1)
    for g in range(N_GMLP_GROUPS):
        gs = slice(g * GMLP_GROUP_DIM, (g + 1) * GMLP_GROUP_DIM)
        vn = _layer_norm(vg[:, gs], lng_ref[:, gs], lnb_ref[:, gs]).astype(BF16)
        rhs = jnp.concatenate(
            [vn[c * GMLP_CHUNK:(c + 1) * GMLP_CHUNK] for c in range(n_chunks)], axis=1)
        w_causal = jnp.where(t_idx >= s_idx, wsp_ref[g], 0.0).astype(BF16)
        gate = jnp.dot(w_causal, rhs, preferred_element_type=F32)
        bias = bsp_ref[:, g:g + 1]
        for c in range(n_chunks):
            rs = slice(c * GMLP_CHUNK, (c + 1) * GMLP_CHUNK)
            gate_c = gate[:, c * GMLP_GROUP_DIM:(c + 1) * GMLP_GROUP_DIM] + bias
            ob_ref[0, rs, gs] = (u[rs, gs] * gate_c).astype(BF16)


def _attn_kernel(q_ref, k_ref, vt_ref, lq1_ref, lk1_ref, lq2_ref, lk2_ref, g_ref,
                 o_ref, m_ref, acc_ref, sa_ref, sb_ref, sc_ref, ma_ref, mb_ref, mc_ref,
                 *, lambda_init):
    tk = vt_ref.shape[4]
    tq = tk
    n_q = q_ref.shape[2] // tq
    groups = range(0, 2 * tq, ATT_COLS)

    def scores(qi, kt, s_ref, mx_ref, diagonal):
        q = q_ref[0, :, pl.ds(pl.multiple_of(qi * tq, tq), tq), :].reshape(2 * tq, PAIR_DIM)
        k = k_ref[0, 0, pl.ds(pl.multiple_of(kt * tk, tk), tk), :]
        for c0 in groups:
            cols = slice(c0, c0 + ATT_COLS)
            s = lax.dot_general(k, q[cols], (((1,), (1,)), ((), ())),
                                preferred_element_type=F32)
            if diagonal:
                k_chunk = lax.broadcasted_iota(jnp.int32, s.shape, 0) // CHUNK
                q_pos = lax.broadcasted_iota(jnp.int32, s.shape, 1) + (c0 & (tq - 1))
                s = jnp.where(k_chunk <= q_pos // CHUNK, s, MASKED_SCORE)
            s_ref[:, cols] = s
            mx_ref[:, cols] = jnp.max(s, axis=0, keepdims=True)

    def update(kt, s_ref, mx_ref):
        vt = vt_ref[0, 0, kt]
        for c0 in groups:
            cols = slice(c0, c0 + ATT_COLS)
            m_prev = m_ref[:, cols]
            m_new = jnp.maximum(m_prev, mx_ref[:, cols])
            alpha = jnp.exp2(m_prev - m_new)
            p = jnp.exp2(s_ref[:, cols] - m_new)
            acc_ref[:, cols] = alpha * acc_ref[:, cols] + jnp.dot(
                vt, p.astype(BF16), preferred_element_type=F32)
            m_ref[:, cols] = m_new

    def start_tile():
        m_ref[...] = jnp.full(m_ref.shape, MASKED_SCORE, F32)
        acc_ref[...] = jnp.zeros(acc_ref.shape, F32)

    lam = (jnp.exp(jnp.sum(lq1_ref[...] * lk1_ref[...], axis=1, keepdims=True))
           - jnp.exp(jnp.sum(lq2_ref[...] * lk2_ref[...], axis=1, keepdims=True))
           + lambda_init)

    def finish_tile(qi):
        o_all = acc_ref[:PAIR_DIM, :] / acc_ref[PAIR_DIM:PAIR_DIM + 1, :]
        o = o_all[:, :tq] - lam * o_all[:, tq:]
        o = o * lax.rsqrt(jnp.mean(o * o, axis=0, keepdims=True) + LN_EPS)
        o = o.T * (g_ref[...] * (1.0 - lambda_init))
        o_ref[0, pl.ds(pl.multiple_of(qi * tq, tq), tq), :] = o.astype(o_ref.dtype)

    start_tile()
    scores(0, 0, sc_ref, mc_ref, True)
    update(0, sc_ref, mc_ref)
    finish_tile(0)
    start_tile()
    second = min(1, n_q - 1)
    scores(second, second, sc_ref, mc_ref, True)

    def query_tile(i, carry):
        nxt = jnp.minimum(i + 1, n_q - 1)
        scores(i, 0, sa_ref, ma_ref, False)
        update(i, sc_ref, mc_ref)

        def two_tiles(t, c):
            scores(i, 2 * t + 1, sb_ref, mb_ref, False)
            update(2 * t, sa_ref, ma_ref)
            scores(i, 2 * t + 2, sa_ref, ma_ref, False)
            update(2 * t + 1, sb_ref, mb_ref)
            return c

        lax.fori_loop(0, (i - 1) // 2, two_tiles, 0)

        @pl.when(i % 2 == 1)
        def _():
            scores(nxt, nxt, sc_ref, mc_ref, True)
            update(i - 1, sa_ref, ma_ref)

        @pl.when(i % 2 == 0)
        def _():
            scores(i, i - 1, sb_ref, mb_ref, False)
            update(i - 2, sa_ref, ma_ref)
            scores(nxt, nxt, sc_ref, mc_ref, True)
            update(i - 1, sb_ref, mb_ref)

        finish_tile(i)
        start_tile()
        return carry

    lax.fori_loop(1, n_q, query_tile, 0)


def _out_ffn_kernel(x_ref, oa_ref, ob_ref, wo_ref, g1_ref, b1_ref, wg_ref, wu_ref,
                    cw_ref, cb_ref, wd_ref, g2_ref, b2_ref, out_ref, gate_ref, *, alpha):
    rows = x_ref.shape[1]
    d_ff = wg_ref.shape[1]
    seq_tile = pl.program_id(1)

    cat = jnp.concatenate([oa_ref[0], ob_ref[0]], axis=1)
    mix = jnp.dot(cat, wo_ref[...], preferred_element_type=F32)
    y1 = _layer_norm(alpha * x_ref[0] + mix, g1_ref[...], b1_ref[...])
    yb = y1.astype(BF16)

    @pl.when(seq_tile == 0)
    def _():
        gate_ref[:SUBLANES, :] = jnp.zeros((SUBLANES, d_ff), F32)

    f = jnp.zeros((rows, x_ref.shape[2]), F32)
    for c0 in range(0, d_ff, FFN_COLS):
        cs = slice(c0, min(c0 + FFN_COLS, d_ff))
        g = jnp.dot(yb, wg_ref[:, cs], preferred_element_type=F32)
        up = jnp.dot(yb, wu_ref[:, cs], preferred_element_type=F32)
        gate_ref[SUBLANES:, cs] = g
        conv = cb_ref[:, cs] + cw_ref[CONV_WIDTH - 1:CONV_WIDTH, cs] * g
        for tap in range(CONV_WIDTH - 1):
            back = CONV_WIDTH - 1 - tap
            conv = conv + cw_ref[tap:tap + 1, cs] * gate_ref[SUBLANES - back:SUBLANES - back + rows, cs]
        gate_ref[:SUBLANES, cs] = g[rows - SUBLANES:]
        half = 0.5 * conv
        h = ((half + half * jnp.tanh(half)) * up).astype(BF16)
        f = f + jnp.dot(h, wd_ref[cs, :], preferred_element_type=F32)
    out_ref[0] = _layer_norm(alpha * y1 + f, g2_ref[...], b2_ref[...])


def _rope_tables(seq_len):
    pos = jnp.arange(seq_len, dtype=F32)
    inv_freq = 1.0 / (ROPE_THETA ** (jnp.arange(0, DIFF_HEAD_DIM, 2, dtype=F32) / DIFF_HEAD_DIM))
    ang = pos[:, None] * inv_freq[None, :]
    cos, sin = jnp.cos(ang), jnp.sin(ang)
    reps = LANES // (DIFF_HEAD_DIM // 2)
    cos_t = jnp.tile(cos, (1, reps))
    sin_t = jnp.tile(jnp.concatenate([-sin, sin], axis=1), (1, reps // 2))
    return cos_t, sin_t


def _const_spec(shape):
    return pl.BlockSpec(shape, lambda *_: (0,) * len(shape), pipeline_mode=pl.Buffered(1))


def _layer(x, cos_t, sin_t, lambda_init, alpha, w_in, lq1, lk1, lq2, lk2, subln_g, lng, lnb,
           w_sp, b_sp, w_out, ln1_g, ln1_b, w_gate, w_up, conv_w, conv_b, w_down, ln2_g, ln2_b):
    B, S, D = x.shape
    att_width = N_DIFF_HEADS * PAIR_DIM
    gmlp_width = N_GMLP_GROUPS * GMLP_GROUP_DIM
    d_ff = w_gate.shape[1]
    n_pairs = N_DIFF_HEADS
    row = lambda a: a.reshape(1, -1)

    q, k, v, ob = pl.pallas_call(
        functools.partial(_in_proj_kernel, att_width=att_width, gmlp_width=gmlp_width),
        grid=(B, S // IN_ROWS),
        in_specs=[
            pl.BlockSpec((1, IN_ROWS, D), lambda b, i: (b, i, 0)),
            _const_spec(w_in.shape),
            pl.BlockSpec((IN_ROWS, LANES), lambda b, i: (i, 0)),
            pl.BlockSpec((IN_ROWS, LANES), lambda b, i: (i, 0)),
            _const_spec((1, gmlp_width)),
            _const_spec((1, gmlp_width)),
            _const_spec(w_sp.shape),
            _const_spec((GMLP_CHUNK, N_GMLP_GROUPS)),
        ],
        out_specs=[
            pl.BlockSpec((1, 2 * n_pairs, IN_ROWS, PAIR_DIM), lambda b, i: (b, 0, i, 0)),
            pl.BlockSpec((1, n_pairs, IN_ROWS, PAIR_DIM), lambda b, i: (b, 0, i, 0)),
            pl.BlockSpec((1, n_pairs, 1, VT_ROWS, IN_ROWS), lambda b, i: (b, 0, i, 0, 0)),
            pl.BlockSpec((1, IN_ROWS, gmlp_width), lambda b, i: (b, i, 0)),
        ],
        out_shape=[
            jax.ShapeDtypeStruct((B, 2 * n_pairs, S, PAIR_DIM), BF16),
            jax.ShapeDtypeStruct((B, n_pairs, S, PAIR_DIM), BF16),
            jax.ShapeDtypeStruct((B, n_pairs, S // IN_ROWS, VT_ROWS, IN_ROWS), BF16),
            jax.ShapeDtypeStruct((B, S, gmlp_width), BF16),
        ],
        compiler_params=pltpu.CompilerParams(
            dimension_semantics=("arbitrary", "arbitrary"), vmem_limit_bytes=VMEM_LIMIT),
        name="in_proj_gmlp",
    )(x, w_in.astype(BF16), cos_t, sin_t, row(lng), row(lnb), w_sp, b_sp.T)

    oa = pl.pallas_call(
        functools.partial(_attn_kernel, lambda_init=lambda_init),
        grid=(B, n_pairs),
        in_specs=[
            pl.BlockSpec((1, 2, S, PAIR_DIM), lambda b, p: (b, p, 0, 0)),
            pl.BlockSpec((1, 1, S, PAIR_DIM), lambda b, p: (b, p, 0, 0)),
            pl.BlockSpec((1, 1, S // IN_ROWS, VT_ROWS, IN_ROWS), lambda b, p: (b, p, 0, 0, 0)),
            _const_spec((1, DIFF_HEAD_DIM)),
            _const_spec((1, DIFF_HEAD_DIM)),
            _const_spec((1, DIFF_HEAD_DIM)),
            _const_spec((1, DIFF_HEAD_DIM)),
            _const_spec((1, PAIR_DIM)),
        ],
        out_specs=pl.BlockSpec((1, S, PAIR_DIM), lambda b, p: (b, 0, p)),
        out_shape=jax.ShapeDtypeStruct((B, S, att_width), BF16),
        scratch_shapes=[
            pltpu.VMEM((1, 2 * IN_ROWS), F32),
            pltpu.VMEM((VT_ROWS, 2 * IN_ROWS), F32),
            pltpu.VMEM((IN_ROWS, 2 * IN_ROWS), F32),
            pltpu.VMEM((IN_ROWS, 2 * IN_ROWS), F32),
            pltpu.VMEM((IN_ROWS, 2 * IN_ROWS), F32),
            pltpu.VMEM((1, 2 * IN_ROWS), F32),
            pltpu.VMEM((1, 2 * IN_ROWS), F32),
            pltpu.VMEM((1, 2 * IN_ROWS), F32),
        ],
        compiler_params=pltpu.CompilerParams(
            dimension_semantics=("arbitrary", "arbitrary"),
            vmem_limit_bytes=VMEM_LIMIT),
        name="diff_attention",
    )(q, k, v, row(lq1), row(lk1), row(lq2), row(lk2), row(subln_g))

    out = pl.pallas_call(
        functools.partial(_out_ffn_kernel, alpha=alpha),
        grid=(B, S // FFN_ROWS),
        in_specs=[
            pl.BlockSpec((1, FFN_ROWS, D), lambda b, i: (b, i, 0)),
            pl.BlockSpec((1, FFN_ROWS, att_width), lambda b, i: (b, i, 0)),
            pl.BlockSpec((1, FFN_ROWS, gmlp_width), lambda b, i: (b, i, 0)),
            _const_spec(w_out.shape),
            _const_spec((1, D)),
            _const_spec((1, D)),
            _const_spec(w_gate.shape),
            _const_spec(w_up.shape),
            _const_spec(conv_w.shape),
            _const_spec((1, d_ff)),
            _const_spec(w_down.shape),
            _const_spec((1, D)),
            _const_spec((1, D)),
        ],
        out_specs=pl.BlockSpec((1, FFN_ROWS, D), lambda b, i: (b, i, 0)),
        out_shape=jax.ShapeDtypeStruct((B, S, D), x.dtype),
        scratch_shapes=[pltpu.VMEM((SUBLANES + FFN_ROWS, d_ff), F32)],
        compiler_params=pltpu.CompilerParams(
            dimension_semantics=("arbitrary", "arbitrary"), vmem_limit_bytes=VMEM_LIMIT),
        name="out_proj_ffn",
    )(x, oa, ob, w_out.astype(BF16), row(ln1_g), row(ln1_b), w_gate.astype(BF16),
      w_up.astype(BF16), conv_w, row(conv_b), w_down.astype(BF16), row(ln2_g), row(ln2_b))
    return out


def kernel(x, w_in, lambda_q1, lambda_k1, lambda_q2, lambda_k2, subln_g, gmlp_ln_g, gmlp_ln_b,
           w_spatial, b_spatial, w_out, ln1_g, ln1_b, w_gate, w_up, conv_w, conv_b, w_down,
           ln2_g, ln2_b):
    depth = w_in.shape[0]
    alpha = (2 * depth) ** 0.25
    cos_t, sin_t = _rope_tables(x.shape[1])
    for l in range(depth):
        lambda_init = 0.8 - 0.6 * math.exp(-0.3 * l)
        x = _layer(x, cos_t, sin_t, lambda_init, alpha, w_in[l], lambda_q1[l], lambda_k1[l],
                   lambda_q2[l], lambda_k2[l], subln_g[l], gmlp_ln_g[l], gmlp_ln_b[l],
                   w_spatial[l], b_spatial[l], w_out[l], ln1_g[l], ln1_b[l], w_gate[l],
                   w_up[l], conv_w[l], conv_b[l], w_down[l], ln2_g[l], ln2_b[l])
    return x
```

```python
import functools
import math

import jax
import jax.numpy as jnp
from jax import lax
from jax.experimental import pallas as pl
from jax.experimental.pallas import tpu as pltpu

F32 = jnp.float32
BF16 = jnp.bfloat16

CHUNK = 64
N_DIFF_HEADS = 4
DIFF_HEAD_DIM = 64
PAIR_DIM = 2 * DIFF_HEAD_DIM
N_GMLP_GROUPS = 4
GMLP_GROUP_DIM = 128
GMLP_CHUNK = 128
CONV_WIDTH = 3
ROPE_THETA = 10000.0
LN_EPS = 1e-5

LANES = 128
SUBLANES = 8
BF16_SUBLANES = 16
VT_ROWS = PAIR_DIM + BF16_SUBLANES
LOG2_E = math.log2(math.e)

IN_ROWS = 512
ATT_COLS = 256
FFN_ROWS = 512
FFN_COLS = 768
MASKED_SCORE = -1e30
VMEM_LIMIT = 56 * 1024 * 1024


def _layer_norm(z, g, b):
    mu = jnp.mean(z, axis=-1, keepdims=True)
    d = z - mu
    var = jnp.mean(d * d, axis=-1, keepdims=True)
    return d * lax.rsqrt(var + LN_EPS) * g + b


def _gelu(z):
    return 0.5 * z * (1.0 + lax.erf(z * (1.0 / math.sqrt(2.0))))


def _in_proj_kernel(x_ref, w_ref, cos_ref, sin_ref, lng_ref, lnb_ref, wsp_ref, bsp_ref,
                    q_ref, k_ref, vt_ref, ob_ref, *, att_width, gmlp_width):
    rows = x_ref.shape[1]
    xb = x_ref[0].astype(BF16)

    def proj(col0, width):
        return jnp.dot(xb, w_ref[:, col0:col0 + width], preferred_element_type=F32)

    cos = cos_ref[...]
    sin = sin_ref[...]
    lane = lax.broadcasted_iota(jnp.int32, (rows, LANES), 1)
    first_half = (lane & (DIFF_HEAD_DIM // 2)) == 0
    low_head = lane < DIFF_HEAD_DIM

    def rope(t):
        rot = jnp.where(first_half, pltpu.roll(t, LANES - DIFF_HEAD_DIM // 2, 1),
                        pltpu.roll(t, DIFF_HEAD_DIM // 2, 1))
        return t * cos + rot * sin

    vg = _gelu(proj(3 * att_width + gmlp_width, gmlp_width))
    u = _gelu(proj(3 * att_width, gmlp_width))
    n_chunks = rows // GMLP_CHUNK
    gate_rhs = []
    for g in range(N_GMLP_GROUPS):
        gs = slice(g * GMLP_GROUP_DIM, (g + 1) * GMLP_GROUP_DIM)
        vn = _layer_norm(vg[:, gs], lng_ref[:, gs], lnb_ref[:, gs]).astype(BF16)
        gate_rhs.append(jnp.concatenate(
            [vn[c * GMLP_CHUNK:(c + 1) * GMLP_CHUNK] for c in range(n_chunks)], axis=1))

    n_pairs = att_width // PAIR_DIM
    hq = proj(0, att_width)
    for p in range(n_pairs):
        r = rope(hq[:, p * PAIR_DIM:(p + 1) * PAIR_DIM]) * (DIFF_HEAD_DIM ** -0.5 * LOG2_E)
        q_ref[0, 2 * p] = jnp.where(low_head, r, 0.0).astype(BF16)
        q_ref[0, 2 * p + 1] = jnp.where(low_head, 0.0, r).astype(BF16)
    hk = proj(att_width, att_width)
    for p in range(n_pairs):
        k_ref[0, p] = rope(hk[:, p * PAIR_DIM:(p + 1) * PAIR_DIM]).astype(BF16)
    hv = proj(2 * att_width, att_width)
    for p in range(n_pairs):
        vt_ref[0, p, 0, :PAIR_DIM, :] = hv[:, p * PAIR_DIM:(p + 1) * PAIR_DIM].T.astype(BF16)
        extra = lax.broadcasted_iota(jnp.int32, (BF16_SUBLANES, rows), 0)
        vt_ref[0, p, 0, PAIR_DIM:, :] = jnp.where(extra == 0, 1.0, 0.0).astype(BF16)

    t_idx = lax.broadcasted_iota(jnp.int32, (GMLP_CHUNK, GMLP_CHUNK), 0)
    s_idx = lax.broadcasted_iota(jnp.int32, (GMLP_CHUNK, GMLP_CHUNK), 1)
    for g in range(N_GMLP_GROUPS):
        gs = slice(g * GMLP_GROUP_DIM, (g + 1) * GMLP_GROUP_DIM)
        w_causal = jnp.where(t_idx >= s_idx, wsp_ref[g], 0.0).astype(BF16)
        gate = jnp.dot(w_causal, gate_rhs[g], preferred_element_type=F32)
        bias = bsp_ref[:, g:g + 1]
        for c in range(n_chunks):
            rs = slice(c * GMLP_CHUNK, (c + 1) * GMLP_CHUNK)
            gate_c = gate[:, c * GMLP_GROUP_DIM:(c + 1) * GMLP_GROUP_DIM] + bias
            ob_ref[0, rs, gs] = (u[rs, gs] * gate_c).astype(BF16)


def _attn_kernel(q_ref, k_ref, vt_ref, lq1_ref, lk1_ref, lq2_ref, lk2_ref, g_ref,
                 o_ref, m_ref, acc_ref, sa_ref, sb_ref, sc_ref, ma_ref, mb_ref, mc_ref,
                 *, lambda_init):
    tk = vt_ref.shape[4]
    tq = tk
    n_q = q_ref.shape[2] // tq
    groups = range(0, 2 * tq, ATT_COLS)

    def scores(qi, kt, s_ref, mx_ref, diagonal):
        q = q_ref[0, :, pl.ds(pl.multiple_of(qi * tq, tq), tq), :].reshape(2 * tq, PAIR_DIM)
        k = k_ref[0, 0, pl.ds(pl.multiple_of(kt * tk, tk), tk), :]
        for c0 in groups:
            cols = slice(c0, c0 + ATT_COLS)
            s = lax.dot_general(k, q[cols], (((1,), (1,)), ((), ())),
                                preferred_element_type=F32)
            if diagonal:
                k_chunk = lax.broadcasted_iota(jnp.int32, s.shape, 0) // CHUNK
                q_pos = lax.broadcasted_iota(jnp.int32, s.shape, 1) + (c0 & (tq - 1))
                s = jnp.where(k_chunk <= q_pos // CHUNK, s, MASKED_SCORE)
            s_ref[:, cols] = s
            mx_ref[:, cols] = jnp.max(s, axis=0, keepdims=True)

    def update(kt, s_ref, mx_ref):
        vt = vt_ref[0, 0, kt]
        for c0 in groups:
            cols = slice(c0, c0 + ATT_COLS)
            m_prev = m_ref[:, cols]
            m_new = jnp.maximum(m_prev, mx_ref[:, cols])
            alpha = jnp.exp2(m_prev - m_new)
            p = jnp.exp2(s_ref[:, cols] - m_new)
            acc_ref[:, cols] = alpha * acc_ref[:, cols] + jnp.dot(
                vt, p.astype(BF16), preferred_element_type=F32)
            m_ref[:, cols] = m_new

    def start_tile():
        m_ref[...] = jnp.full(m_ref.shape, MASKED_SCORE, F32)
        acc_ref[...] = jnp.zeros(acc_ref.shape, F32)

    lam = (jnp.exp(jnp.sum(lq1_ref[...] * lk1_ref[...], axis=1, keepdims=True))
           - jnp.exp(jnp.sum(lq2_ref[...] * lk2_ref[...], axis=1, keepdims=True))
           + lambda_init)

    def finish_tile(qi):
        o_all = acc_ref[:PAIR_DIM, :] / acc_ref[PAIR_DIM:PAIR_DIM + 1, :]
        o = o_all[:, :tq] - lam * o_all[:, tq:]
        o = o * lax.rsqrt(jnp.mean(o * o, axis=0, keepdims=True) + LN_EPS)
        o = o.T * (g_ref[...] * (1.0 - lambda_init))
        o_ref[0, pl.ds(pl.multiple_of(qi * tq, tq), tq), :] = o.astype(o_ref.dtype)

    start_tile()
    scores(0, 0, sc_ref, mc_ref, True)
    update(0, sc_ref, mc_ref)
    second = min(1, n_q - 1)
    scores(second, second, sc_ref, mc_ref, True)

    def query_tile(i, carry):
        nxt = jnp.minimum(i + 1, n_q - 1)
        finish_tile(i - 1)
        start_tile()
        scores(i, 0, sa_ref, ma_ref, False)
        update(i, sc_ref, mc_ref)

        def two_tiles(t, c):
            scores(i, 2 * t + 1, sb_ref, mb_ref, False)
            update(2 * t, sa_ref, ma_ref)
            scores(i, 2 * t + 2, sa_ref, ma_ref, False)
            update(2 * t + 1, sb_ref, mb_ref)
            return c

        lax.fori_loop(0, (i - 1) // 2, two_tiles, 0)

        @pl.when(i % 2 == 1)
        def _():
            scores(nxt, nxt, sc_ref, mc_ref, True)
            update(i - 1, sa_ref, ma_ref)

        @pl.when(i % 2 == 0)
        def _():
            scores(i, i - 1, sb_ref, mb_ref, False)
            update(i - 2, sa_ref, ma_ref)
            scores(nxt, nxt, sc_ref, mc_ref, True)
            update(i - 1, sb_ref, mb_ref)

        return carry

    lax.fori_loop(1, n_q, query_tile, 0)
    finish_tile(n_q - 1)


def _out_ffn_kernel(x_ref, oa_ref, ob_ref, wo_ref, g1_ref, b1_ref, wg_ref, wu_ref,
                    cw_ref, cb_ref, wd_ref, g2_ref, b2_ref, out_ref, gate_ref, *, alpha):
    rows = x_ref.shape[1]
    d_ff = wg_ref.shape[1]
    seq_tile = pl.program_id(1)
    halves = [slice(0, rows // 2), slice(rows // 2, rows)]
    whole = [slice(0, rows)]

    def mixed(rs):
        cat = jnp.concatenate([oa_ref[0, rs], ob_ref[0, rs]], axis=1)
        mix = jnp.dot(cat, wo_ref[...], preferred_element_type=F32)
        return _layer_norm(alpha * x_ref[0, rs] + mix, g1_ref[...], b1_ref[...])

    y1 = jnp.concatenate([mixed(rs) for rs in halves], axis=0)
    yb = y1.astype(BF16)

    @pl.when(seq_tile == 0)
    def _():
        gate_ref[:SUBLANES, :] = jnp.zeros((SUBLANES, d_ff), F32)

    def ffn_chunk(rs, cs, f_rows):
        g = jnp.dot(yb[rs], wg_ref[:, cs], preferred_element_type=F32)
        up = jnp.dot(yb[rs], wu_ref[:, cs], preferred_element_type=F32)
        gate_ref[SUBLANES + rs.start:SUBLANES + rs.stop, cs] = g
        conv = cb_ref[:, cs] + cw_ref[CONV_WIDTH - 1:CONV_WIDTH, cs] * g
        for tap in range(CONV_WIDTH - 1):
            back = CONV_WIDTH - 1 - tap
            window = slice(SUBLANES + rs.start - back, SUBLANES + rs.stop - back)
            conv = conv + cw_ref[tap:tap + 1, cs] * gate_ref[window, cs]
        half = 0.5 * conv
        h = ((half + half * jnp.tanh(half)) * up).astype(BF16)
        return f_rows + jnp.dot(h, wd_ref[cs, :], preferred_element_type=F32)

    f = jnp.zeros((rows, x_ref.shape[2]), F32)
    starts = list(range(0, d_ff, FFN_COLS))
    for c0 in starts:
        cs = slice(c0, min(c0 + FFN_COLS, d_ff))
        parts = halves if c0 in (starts[0], starts[-1]) else whole
        f = jnp.concatenate([ffn_chunk(rs, cs, f[rs]) for rs in parts], axis=0)
        gate_ref[:SUBLANES, cs] = gate_ref[rows:rows + SUBLANES, cs]
    for rs in halves:
        out_ref[0, rs] = _layer_norm(alpha * y1[rs] + f[rs], g2_ref[...], b2_ref[...])


def _rope_tables(seq_len):
    pos = jnp.arange(seq_len, dtype=F32)
    inv_freq = 1.0 / (ROPE_THETA ** (jnp.arange(0, DIFF_HEAD_DIM, 2, dtype=F32) / DIFF_HEAD_DIM))
    ang = pos[:, None] * inv_freq[None, :]
    cos, sin = jnp.cos(ang), jnp.sin(ang)
    reps = LANES // (DIFF_HEAD_DIM // 2)
    cos_t = jnp.tile(cos, (1, reps))
    sin_t = jnp.tile(jnp.concatenate([-sin, sin], axis=1), (1, reps // 2))
    return cos_t, sin_t


def _const_spec(shape):
    return pl.BlockSpec(shape, lambda *_: (0,) * len(shape), pipeline_mode=pl.Buffered(1))


def _layer(x, cos_t, sin_t, lambda_init, alpha, w_in, lq1, lk1, lq2, lk2, subln_g, lng, lnb,
           w_sp, b_sp, w_out, ln1_g, ln1_b, w_gate, w_up, conv_w, conv_b, w_down, ln2_g, ln2_b):
    B, S, D = x.shape
    att_width = N_DIFF_HEADS * PAIR_DIM
    gmlp_width = N_GMLP_GROUPS * GMLP_GROUP_DIM
    d_ff = w_gate.shape[1]
    n_pairs = N_DIFF_HEADS
    row = lambda a: a.reshape(1, -1)

    q, k, v, ob = pl.pallas_call(
        functools.partial(_in_proj_kernel, att_width=att_width, gmlp_width=gmlp_width),
        grid=(B, S // IN_ROWS),
        in_specs=[
            pl.BlockSpec((1, IN_ROWS, D), lambda b, i: (b, i, 0)),
            _const_spec(w_in.shape),
            pl.BlockSpec((IN_ROWS, LANES), lambda b, i: (i, 0)),
            pl.BlockSpec((IN_ROWS, LANES), lambda b, i: (i, 0)),
            _const_spec((1, gmlp_width)),
            _const_spec((1, gmlp_width)),
            _const_spec(w_sp.shape),
            _const_spec((GMLP_CHUNK, N_GMLP_GROUPS)),
        ],
        out_specs=[
            pl.BlockSpec((1, 2 * n_pairs, IN_ROWS, PAIR_DIM), lambda b, i: (b, 0, i, 0)),
            pl.BlockSpec((1, n_pairs, IN_ROWS, PAIR_DIM), lambda b, i: (b, 0, i, 0)),
            pl.BlockSpec((1, n_pairs, 1, VT_ROWS, IN_ROWS), lambda b, i: (b, 0, i, 0, 0)),
            pl.BlockSpec((1, IN_ROWS, gmlp_width), lambda b, i: (b, i, 0)),
        ],
        out_shape=[
            jax.ShapeDtypeStruct((B, 2 * n_pairs, S, PAIR_DIM), BF16),
            jax.ShapeDtypeStruct((B, n_pairs, S, PAIR_DIM), BF16),
            jax.ShapeDtypeStruct((B, n_pairs, S // IN_ROWS, VT_ROWS, IN_ROWS), BF16),
            jax.ShapeDtypeStruct((B, S, gmlp_width), BF16),
        ],
        compiler_params=pltpu.CompilerParams(
            dimension_semantics=("arbitrary", "arbitrary"), vmem_limit_bytes=VMEM_LIMIT),
        name="in_proj_gmlp",
    )(x, w_in.astype(BF16), cos_t, sin_t, row(lng), row(lnb), w_sp, b_sp.T)

    oa = pl.pallas_call(
        functools.partial(_attn_kernel, lambda_init=lambda_init),
        grid=(B, n_pairs),
        in_specs=[
            pl.BlockSpec((1, 2, S, PAIR_DIM), lambda b, p: (b, p, 0, 0)),
            pl.BlockSpec((1, 1, S, PAIR_DIM), lambda b, p: (b, p, 0, 0)),
            pl.BlockSpec((1, 1, S // IN_ROWS, VT_ROWS, IN_ROWS), lambda b, p: (b, p, 0, 0, 0)),
            _const_spec((1, DIFF_HEAD_DIM)),
            _const_spec((1, DIFF_HEAD_DIM)),
            _const_spec((1, DIFF_HEAD_DIM)),
            _const_spec((1, DIFF_HEAD_DIM)),
            _const_spec((1, PAIR_DIM)),
        ],
        out_specs=pl.BlockSpec((1, S, PAIR_DIM), lambda b, p: (b, 0, p)),
        out_shape=jax.ShapeDtypeStruct((B, S, att_width), BF16),
        scratch_shapes=[
            pltpu.VMEM((1, 2 * IN_ROWS), F32),
            pltpu.VMEM((VT_ROWS, 2 * IN_ROWS), F32),
            pltpu.VMEM((IN_ROWS, 2 * IN_ROWS), F32),
            pltpu.VMEM((IN_ROWS, 2 * IN_ROWS), F32),
            pltpu.VMEM((IN_ROWS, 2 * IN_ROWS), F32),
            pltpu.VMEM((1, 2 * IN_ROWS), F32),
            pltpu.VMEM((1, 2 * IN_ROWS), F32),
            pltpu.VMEM((1, 2 * IN_ROWS), F32),
        ],
        compiler_params=pltpu.CompilerParams(
            dimension_semantics=("arbitrary", "arbitrary"),
            vmem_limit_bytes=VMEM_LIMIT),
        name="diff_attention",
    )(q, k, v, row(lq1), row(lk1), row(lq2), row(lk2), row(subln_g))

    out = pl.pallas_call(
        functools.partial(_out_ffn_kernel, alpha=alpha),
        grid=(B, S // FFN_ROWS),
        in_specs=[
            pl.BlockSpec((1, FFN_ROWS, D), lambda b, i: (b, i, 0)),
            pl.BlockSpec((1, FFN_ROWS, att_width), lambda b, i: (b, i, 0)),
            pl.BlockSpec((1, FFN_ROWS, gmlp_width), lambda b, i: (b, i, 0)),
            _const_spec(w_out.shape),
            _const_spec((1, D)),
            _const_spec((1, D)),
            _const_spec(w_gate.shape),
            _const_spec(w_up.shape),
            _const_spec(conv_w.shape),
            _const_spec((1, d_ff)),
            _const_spec(w_down.shape),
            _const_spec((1, D)),
            _const_spec((1, D)),
        ],
        out_specs=pl.BlockSpec((1, FFN_ROWS, D), lambda b, i: (b, i, 0)),
        out_shape=jax.ShapeDtypeStruct((B, S, D), x.dtype),
        scratch_shapes=[pltpu.VMEM((SUBLANES + FFN_ROWS, d_ff), F32)],
        compiler_params=pltpu.CompilerParams(
            dimension_semantics=("arbitrary", "arbitrary"), vmem_limit_bytes=VMEM_LIMIT),
        name="out_proj_ffn",
    )(x, oa, ob, w_out.astype(BF16), row(ln1_g), row(ln1_b), w_gate.astype(BF16),
      w_up.astype(BF16), conv_w, row(conv_b), w_down.astype(BF16), row(ln2_g), row(ln2_b))
    return out


def kernel(x, w_in, lambda_q1, lambda_k1, lambda_q2, lambda_k2, subln_g, gmlp_ln_g, gmlp_ln_b,
           w_spatial, b_spatial, w_out, ln1_g, ln1_b, w_gate, w_up, conv_w, conv_b, w_down,
           ln2_g, ln2_b):
    depth = w_in.shape[0]
    alpha = (2 * depth) ** 0.25
    cos_t, sin_t = _rope_tables(x.shape[1])
    for l in range(depth):
        lambda_init = 0.8 - 0.6 * math.exp(-0.3 * l)
        x = _layer(x, cos_t, sin_t, lambda_init, alpha, w_in[l], lambda_q1[l], lambda_k1[l],
                   lambda_q2[l], lambda_k2[l], subln_g[l], gmlp_ln_g[l], gmlp_ln_b[l],
                   w_spatial[l], b_spatial[l], w_out[l], ln1_g[l], ln1_b[l], w_gate[l],
                   w_up[l], conv_w[l], conv_b[l], w_down[l], ln2_g[l], ln2_b[l])
    return x
```

```python
import functools
import math
from typing import Any, NamedTuple

import jax
import jax.numpy as jnp
from jax import lax
from jax.experimental import pallas as pl
from jax.experimental.pallas import tpu as pltpu

F32 = jnp.float32
BF16 = jnp.bfloat16

CHUNK = 64
N_DIFF_HEADS = 4
DIFF_HEAD_DIM = 64
PAIR_DIM = 2 * DIFF_HEAD_DIM
N_GMLP_GROUPS = 4
GMLP_GROUP_DIM = 128
GMLP_CHUNK = 128
CONV_WIDTH = 3
ROPE_THETA = 10000.0
LN_EPS = 1e-5

LANES = 128
SUBLANES = 8
BF16_SUBLANES = 16
VT_ROWS = PAIR_DIM + BF16_SUBLANES
LOG2_E = math.log2(math.e)

IN_ROWS = 512
ATT_COLS = 256
ATT_UNROLL = 4
FFN_ROWS = 512
FFN_COLS = 768
MASKED_SCORE = -1e30
VMEM_LIMIT = 56 * 1024 * 1024


class _Tile(NamedTuple):
    value: Any
    parity: int


def _layer_norm(z, g, b):
    mu = jnp.mean(z, axis=-1, keepdims=True)
    d = z - mu
    var = jnp.mean(d * d, axis=-1, keepdims=True)
    return d * lax.rsqrt(var + LN_EPS) * g + b


def _gelu(z):
    return 0.5 * z * (1.0 + lax.erf(z * (1.0 / math.sqrt(2.0))))


def _in_proj_kernel(x_ref, w_ref, cos_ref, sin_ref, lng_ref, lnb_ref, wsp_ref, bsp_ref,
                    q_ref, k_ref, vt_ref, ob_ref, *, att_width, gmlp_width):
    rows = x_ref.shape[1]
    xb = x_ref[0].astype(BF16)

    def proj(col0, width):
        return jnp.dot(xb, w_ref[:, col0:col0 + width], preferred_element_type=F32)

    cos = cos_ref[...]
    sin = sin_ref[...]
    lane = lax.broadcasted_iota(jnp.int32, (rows, LANES), 1)
    first_half = (lane & (DIFF_HEAD_DIM // 2)) == 0
    low_head = lane < DIFF_HEAD_DIM

    def rope(t):
        rot = jnp.where(first_half, pltpu.roll(t, LANES - DIFF_HEAD_DIM // 2, 1),
                        pltpu.roll(t, DIFF_HEAD_DIM // 2, 1))
        return t * cos + rot * sin

    vg = _gelu(proj(3 * att_width + gmlp_width, gmlp_width))
    u = _gelu(proj(3 * att_width, gmlp_width))
    n_chunks = rows // GMLP_CHUNK
    gate_rhs = []
    for g in range(N_GMLP_GROUPS):
        gs = slice(g * GMLP_GROUP_DIM, (g + 1) * GMLP_GROUP_DIM)
        vn = _layer_norm(vg[:, gs], lng_ref[:, gs], lnb_ref[:, gs]).astype(BF16)
        gate_rhs.append(jnp.concatenate(
            [vn[c * GMLP_CHUNK:(c + 1) * GMLP_CHUNK] for c in range(n_chunks)], axis=1))

    n_pairs = att_width // PAIR_DIM
    hq = proj(0, att_width)
    for p in range(n_pairs):
        r = rope(hq[:, p * PAIR_DIM:(p + 1) * PAIR_DIM]) * (DIFF_HEAD_DIM ** -0.5 * LOG2_E)
        q_ref[0, 2 * p] = jnp.where(low_head, r, 0.0).astype(BF16)
        q_ref[0, 2 * p + 1] = jnp.where(low_head, 0.0, r).astype(BF16)
    hk = proj(att_width, att_width)
    for p in range(n_pairs):
        k_ref[0, p] = rope(hk[:, p * PAIR_DIM:(p + 1) * PAIR_DIM]).astype(BF16)
    hv = proj(2 * att_width, att_width)
    for p in range(n_pairs):
        vt_ref[0, p, 0, :PAIR_DIM, :] = hv[:, p * PAIR_DIM:(p + 1) * PAIR_DIM].T.astype(BF16)
        extra = lax.broadcasted_iota(jnp.int32, (BF16_SUBLANES, rows), 0)
        vt_ref[0, p, 0, PAIR_DIM:, :] = jnp.where(extra == 0, 1.0, 0.0).astype(BF16)

    t_idx = lax.broadcasted_iota(jnp.int32, (GMLP_CHUNK, GMLP_CHUNK), 0)
    s_idx = lax.broadcasted_iota(jnp.int32, (GMLP_CHUNK, GMLP_CHUNK), 1)
    for g in range(N_GMLP_GROUPS):
        gs = slice(g * GMLP_GROUP_DIM, (g + 1) * GMLP_GROUP_DIM)
        w_causal = jnp.where(t_idx >= s_idx, wsp_ref[g], 0.0).astype(BF16)
        gate = jnp.dot(w_causal, gate_rhs[g], preferred_element_type=F32)
        bias = bsp_ref[:, g:g + 1]
        for c in range(n_chunks):
            rs = slice(c * GMLP_CHUNK, (c + 1) * GMLP_CHUNK)
            gate_c = gate[:, c * GMLP_GROUP_DIM:(c + 1) * GMLP_GROUP_DIM] + bias
            ob_ref[0, rs, gs] = (u[rs, gs] * gate_c).astype(BF16)


def _attn_kernel(q_ref, k_ref, vt_ref, lq1_ref, lk1_ref, lq2_ref, lk2_ref, g_ref,
                 o_ref, m_ref, acc_ref, sa_ref, sb_ref, sc_ref, ma_ref, mb_ref, mc_ref,
                 *, lambda_init):
    tk = vt_ref.shape[4]
    tq = tk
    n_q = q_ref.shape[2] // tq
    groups = range(0, 2 * tq, ATT_COLS)

    def scores(qi, kt, s_ref, mx_ref, diagonal):
        q = q_ref[0, :, pl.ds(pl.multiple_of(qi * tq, tq), tq), :].reshape(2 * tq, PAIR_DIM)
        k = k_ref[0, 0, pl.ds(pl.multiple_of(kt * tk, tk), tk), :]
        for g, c0 in enumerate(groups):
            cols = slice(c0, c0 + ATT_COLS)
            s = lax.dot_general(k, q[cols], (((1,), (1,)), ((), ())),
                                preferred_element_type=F32)
            if diagonal:
                k_chunk = lax.broadcasted_iota(jnp.int32, s.shape, 0) // CHUNK
                q_pos = lax.broadcasted_iota(jnp.int32, s.shape, 1) + (c0 & (tq - 1))
                s = jnp.where(k_chunk <= q_pos // CHUNK, s, MASKED_SCORE)
            s_ref[g] = s
            mx_ref[:, cols] = jnp.max(s, axis=0, keepdims=True)

    def update(kt, s_ref, mx_ref):
        vt = vt_ref[0, 0, kt]
        for g, c0 in enumerate(groups):
            cols = slice(c0, c0 + ATT_COLS)
            m_prev = m_ref[:, cols]
            m_new = jnp.maximum(m_prev, mx_ref[:, cols])
            alpha = jnp.exp2(m_prev - m_new)
            p = jnp.exp2(s_ref[g] - m_new)
            acc_ref[g] = alpha * acc_ref[g] + jnp.dot(
                vt, p.astype(BF16), preferred_element_type=F32)
            m_ref[:, cols] = m_new

    def start_tile():
        m_ref[...] = jnp.full(m_ref.shape, MASKED_SCORE, F32)
        acc_ref[...] = jnp.zeros(acc_ref.shape, F32)

    lam = (jnp.exp(jnp.sum(lq1_ref[...] * lk1_ref[...], axis=1, keepdims=True))
           - jnp.exp(jnp.sum(lq2_ref[...] * lk2_ref[...], axis=1, keepdims=True))
           + lambda_init)

    def finish_tile(qi):
        acc = jnp.concatenate([acc_ref[g] for g in range(len(groups))], axis=1)
        o_all = acc[:PAIR_DIM] / acc[PAIR_DIM:PAIR_DIM + 1]
        o = o_all[:, :tq] - lam * o_all[:, tq:]
        o = o * lax.rsqrt(jnp.mean(o * o, axis=0, keepdims=True) + LN_EPS)
        o = o.T * (g_ref[...] * (1.0 - lambda_init))
        o_ref[0, pl.ds(pl.multiple_of(qi * tq, tq), tq), :] = o.astype(o_ref.dtype)

    start_tile()
    scores(0, 0, sc_ref, mc_ref, True)
    update(0, sc_ref, mc_ref)
    second = min(1, n_q - 1)
    scores(second, second, sc_ref, mc_ref, True)

    def query_tile(i, carry):
        nxt = jnp.minimum(i + 1, n_q - 1)
        finish_tile(i - 1)
        start_tile()
        scores(i, 0, sa_ref, ma_ref, False)
        update(i, sc_ref, mc_ref)
        bufs = ((sa_ref, ma_ref), (sb_ref, mb_ref))

        def next_scores_then_update(j):
            scores(i, j.value + 1, *bufs[(j.parity + 1) % 2], False)
            update(j.value, *bufs[j.parity])

        def unrolled(t, c):
            for u in range(ATT_UNROLL):
                next_scores_then_update(_Tile(ATT_UNROLL * t + u, u % 2))
            return c

        done = (i - 1) // ATT_UNROLL
        lax.fori_loop(0, done, unrolled, 0)

        base = ATT_UNROLL * done
        for left in range(ATT_UNROLL):
            @pl.when((i - 1) % ATT_UNROLL == left)
            def _(left=left):
                for u in range(left):
                    next_scores_then_update(_Tile(base + u, u % 2))
                scores(nxt, nxt, sc_ref, mc_ref, True)
                update(base + left, *bufs[left % 2])

        return carry

    lax.fori_loop(1, n_q, query_tile, 0)
    finish_tile(n_q - 1)


def _out_ffn_kernel(x_ref, oa_ref, ob_ref, wo_ref, g1_ref, b1_ref, wg_ref, wu_ref,
                    cw_ref, cb_ref, wd_ref, g2_ref, b2_ref, out_ref, gate_ref, *, alpha):
    rows = x_ref.shape[1]
    d_ff = wg_ref.shape[1]
    seq_tile = pl.program_id(1)
    halves = [slice(0, rows // 2), slice(rows // 2, rows)]
    whole = [slice(0, rows)]

    def mixed(rs):
        cat = jnp.concatenate([oa_ref[0, rs], ob_ref[0, rs]], axis=1)
        mix = jnp.dot(cat, wo_ref[...], preferred_element_type=F32)
        return _layer_norm(alpha * x_ref[0, rs] + mix, g1_ref[...], b1_ref[...])

    y1 = jnp.concatenate([mixed(rs) for rs in halves], axis=0)
    yb = y1.astype(BF16)

    @pl.when(seq_tile == 0)
    def _():
        gate_ref[:SUBLANES, :] = jnp.zeros((SUBLANES, d_ff), F32)

    def ffn_chunk(rs, cs, f_rows):
        g = jnp.dot(yb[rs], wg_ref[:, cs], preferred_element_type=F32)
        up = jnp.dot(yb[rs], wu_ref[:, cs], preferred_element_type=F32)
        gate_ref[SUBLANES + rs.start:SUBLANES + rs.stop, cs] = g
        conv = cb_ref[:, cs] + cw_ref[CONV_WIDTH - 1:CONV_WIDTH, cs] * g
        for tap in range(CONV_WIDTH - 1):
            back = CONV_WIDTH - 1 - tap
            window = slice(SUBLANES + rs.start - back, SUBLANES + rs.stop - back)
            conv = conv + cw_ref[tap:tap + 1, cs] * gate_ref[window, cs]
        half = 0.5 * conv
        h = ((half + half * jnp.tanh(half)) * up).astype(BF16)
        return f_rows + jnp.dot(h, wd_ref[cs, :], preferred_element_type=F32)

    f = jnp.zeros((rows, x_ref.shape[2]), F32)
    starts = list(range(0, d_ff, FFN_COLS))
    for c0 in starts:
        cs = slice(c0, min(c0 + FFN_COLS, d_ff))
        parts = halves if c0 in (starts[0], starts[-1]) else whole
        f = jnp.concatenate([ffn_chunk(rs, cs, f[rs]) for rs in parts], axis=0)
        gate_ref[:SUBLANES, cs] = gate_ref[rows:rows + SUBLANES, cs]
    for rs in halves:
        out_ref[0, rs] = _layer_norm(alpha * y1[rs] + f[rs], g2_ref[...], b2_ref[...])


def _rope_tables(seq_len):
    pos = jnp.arange(seq_len, dtype=F32)
    inv_freq = 1.0 / (ROPE_THETA ** (jnp.arange(0, DIFF_HEAD_DIM, 2, dtype=F32) / DIFF_HEAD_DIM))
    ang = pos[:, None] * inv_freq[None, :]
    cos, sin = jnp.cos(ang), jnp.sin(ang)
    reps = LANES // (DIFF_HEAD_DIM // 2)
    cos_t = jnp.tile(cos, (1, reps))
    sin_t = jnp.tile(jnp.concatenate([-sin, sin], axis=1), (1, reps // 2))
    return cos_t, sin_t


def _const_spec(shape):
    return pl.BlockSpec(shape, lambda *_: (0,) * len(shape), pipeline_mode=pl.Buffered(1))


def _layer(x, cos_t, sin_t, lambda_init, alpha, w_in, lq1, lk1, lq2, lk2, subln_g, lng, lnb,
           w_sp, b_sp, w_out, ln1_g, ln1_b, w_gate, w_up, conv_w, conv_b, w_down, ln2_g, ln2_b):
    B, S, D = x.shape
    att_width = N_DIFF_HEADS * PAIR_DIM
    gmlp_width = N_GMLP_GROUPS * GMLP_GROUP_DIM
    d_ff = w_gate.shape[1]
    n_pairs = N_DIFF_HEADS
    n_groups = 2 * IN_ROWS // ATT_COLS
    row = lambda a: a.reshape(1, -1)

    q, k, v, ob = pl.pallas_call(
        functools.partial(_in_proj_kernel, att_width=att_width, gmlp_width=gmlp_width),
        grid=(B, S // IN_ROWS),
        in_specs=[
            pl.BlockSpec((1, IN_ROWS, D), lambda b, i: (b, i, 0)),
            _const_spec(w_in.shape),
            pl.BlockSpec((IN_ROWS, LANES), lambda b, i: (i, 0)),
            pl.BlockSpec((IN_ROWS, LANES), lambda b, i: (i, 0)),
            _const_spec((1, gmlp_width)),
            _const_spec((1, gmlp_width)),
            _const_spec(w_sp.shape),
            _const_spec((GMLP_CHUNK, N_GMLP_GROUPS)),
        ],
        out_specs=[
            pl.BlockSpec((1, 2 * n_pairs, IN_ROWS, PAIR_DIM), lambda b, i: (b, 0, i, 0)),
            pl.BlockSpec((1, n_pairs, IN_ROWS, PAIR_DIM), lambda b, i: (b, 0, i, 0)),
            pl.BlockSpec((1, n_pairs, 1, VT_ROWS, IN_ROWS), lambda b, i: (b, 0, i, 0, 0)),
            pl.BlockSpec((1, IN_ROWS, gmlp_width), lambda b, i: (b, i, 0)),
        ],
        out_shape=[
            jax.ShapeDtypeStruct((B, 2 * n_pairs, S, PAIR_DIM), BF16),
            jax.ShapeDtypeStruct((B, n_pairs, S, PAIR_DIM), BF16),
            jax.ShapeDtypeStruct((B, n_pairs, S // IN_ROWS, VT_ROWS, IN_ROWS), BF16),
            jax.ShapeDtypeStruct((B, S, gmlp_width), BF16),
        ],
        compiler_params=pltpu.CompilerParams(
            dimension_semantics=("arbitrary", "arbitrary"), vmem_limit_bytes=VMEM_LIMIT),
        name="in_proj_gmlp",
    )(x, w_in.astype(BF16), cos_t, sin_t, row(lng), row(lnb), w_sp, b_sp.T)

    oa = pl.pallas_call(
        functools.partial(_attn_kernel, lambda_init=lambda_init),
        grid=(B, n_pairs),
        in_specs=[
            pl.BlockSpec((1, 2, S, PAIR_DIM), lambda b, p: (b, p, 0, 0)),
            pl.BlockSpec((1, 1, S, PAIR_DIM), lambda b, p: (b, p, 0, 0)),
            pl.BlockSpec((1, 1, S // IN_ROWS, VT_ROWS, IN_ROWS), lambda b, p: (b, p, 0, 0, 0)),
            _const_spec((1, DIFF_HEAD_DIM)),
            _const_spec((1, DIFF_HEAD_DIM)),
            _const_spec((1, DIFF_HEAD_DIM)),
            _const_spec((1, DIFF_HEAD_DIM)),
            _const_spec((1, PAIR_DIM)),
        ],
        out_specs=pl.BlockSpec((1, S, PAIR_DIM), lambda b, p: (b, 0, p)),
        out_shape=jax.ShapeDtypeStruct((B, S, att_width), BF16),
        scratch_shapes=[
            pltpu.VMEM((1, 2 * IN_ROWS), F32),
            pltpu.VMEM((n_groups, VT_ROWS, ATT_COLS), F32),
            pltpu.VMEM((n_groups, IN_ROWS, ATT_COLS), F32),
            pltpu.VMEM((n_groups, IN_ROWS, ATT_COLS), F32),
            pltpu.VMEM((n_groups, IN_ROWS, ATT_COLS), F32),
            pltpu.VMEM((1, 2 * IN_ROWS), F32),
            pltpu.VMEM((1, 2 * IN_ROWS), F32),
            pltpu.VMEM((1, 2 * IN_ROWS), F32),
        ],
        compiler_params=pltpu.CompilerParams(
            dimension_semantics=("arbitrary", "arbitrary"),
            vmem_limit_bytes=VMEM_LIMIT),
        name="diff_attention",
    )(q, k, v, row(lq1), row(lk1), row(lq2), row(lk2), row(subln_g))

    out = pl.pallas_call(
        functools.partial(_out_ffn_kernel, alpha=alpha),
        grid=(B, S // FFN_ROWS),
        in_specs=[
            pl.BlockSpec((1, FFN_ROWS, D), lambda b, i: (b, i, 0)),
            pl.BlockSpec((1, FFN_ROWS, att_width), lambda b, i: (b, i, 0)),
            pl.BlockSpec((1, FFN_ROWS, gmlp_width), lambda b, i: (b, i, 0)),
            _const_spec(w_out.shape),
            _const_spec((1, D)),
            _const_spec((1, D)),
            _const_spec(w_gate.shape),
            _const_spec(w_up.shape),
            _const_spec(conv_w.shape),
            _const_spec((1, d_ff)),
            _const_spec(w_down.shape),
            _const_spec((1, D)),
            _const_spec((1, D)),
        ],
        out_specs=pl.BlockSpec((1, FFN_ROWS, D), lambda b, i: (b, i, 0)),
        out_shape=jax.ShapeDtypeStruct((B, S, D), x.dtype),
        scratch_shapes=[pltpu.VMEM((SUBLANES + FFN_ROWS, d_ff), F32)],
        compiler_params=pltpu.CompilerParams(
            dimension_semantics=("arbitrary", "arbitrary"), vmem_limit_bytes=VMEM_LIMIT),
        name="out_proj_ffn",
    )(x, oa, ob, w_out.astype(BF16), row(ln1_g), row(ln1_b), w_gate.astype(BF16),
      w_up.astype(BF16), conv_w, row(conv_b), w_down.astype(BF16), row(ln2_g), row(ln2_b))
    return out


def kernel(x, w_in, lambda_q1, lambda_k1, lambda_q2, lambda_k2, subln_g, gmlp_ln_g, gmlp_ln_b,
           w_spatial, b_spatial, w_out, ln1_g, ln1_b, w_gate, w_up, conv_w, conv_b, w_down,
           ln2_g, ln2_b):
    depth = w_in.shape[0]
    alpha = (2 * depth) ** 0.25
    cos_t, sin_t = _rope_tables(x.shape[1])
    for l in range(depth):
        lambda_init = 0.8 - 0.6 * math.exp(-0.3 * l)
        x = _layer(x, cos_t, sin_t, lambda_init, alpha, w_in[l], lambda_q1[l], lambda_k1[l],
                   lambda_q2[l], lambda_k2[l], subln_g[l], gmlp_ln_g[l], gmlp_ln_b[l],
                   w_spatial[l], b_spatial[l], w_out[l], ln1_g[l], ln1_b[l], w_gate[l],
                   w_up[l], conv_w[l], conv_b[l], w_down[l], ln2_g[l], ln2_b[l])
    return x
```

```python
import functools
import math
from typing import Any, NamedTuple

import jax
import jax.numpy as jnp
from jax import lax
from jax.experimental import pallas as pl
from jax.experimental.pallas import tpu as pltpu

F32 = jnp.float32
BF16 = jnp.bfloat16

CHUNK = 64
N_DIFF_HEADS = 4
DIFF_HEAD_DIM = 64
PAIR_DIM = 2 * DIFF_HEAD_DIM
N_GMLP_GROUPS = 4
GMLP_GROUP_DIM = 128
GMLP_CHUNK = 128
CONV_WIDTH = 3
ROPE_THETA = 10000.0
LN_EPS = 1e-5

LANES = 128
SUBLANES = 8
BF16_SUBLANES = 16
VT_ROWS = PAIR_DIM + BF16_SUBLANES
LOG2_E = math.log2(math.e)

IN_ROWS = 512
ATT_COLS = 256
ATT_UNROLL = 4
FFN_ROWS = 512
FFN_COLS = 768
MASKED_SCORE = -1e30
VMEM_LIMIT = 56 * 1024 * 1024


class _Tile(NamedTuple):
    value: Any
    parity: int


def _layer_norm(z, g, b):
    mu = jnp.mean(z, axis=-1, keepdims=True)
    d = z - mu
    var = jnp.mean(d * d, axis=-1, keepdims=True)
    return d * lax.rsqrt(var + LN_EPS) * g + b


def _gelu(z):
    return 0.5 * z * (1.0 + lax.erf(z * (1.0 / math.sqrt(2.0))))


def _in_proj_kernel(x_ref, w_ref, cos_ref, sin_ref, lng_ref, lnb_ref, wsp_ref, bsp_ref,
                    q_ref, k_ref, vt_ref, ob_ref, w16_ref, *, att_width, gmlp_width):
    rows = x_ref.shape[1]
    xb = x_ref[0].astype(BF16)

    @pl.when((pl.program_id(0) == 0) & (pl.program_id(1) == 0))
    def _():
        w16_ref[...] = w_ref[...].astype(BF16)

    def proj(col0, width):
        return jnp.dot(xb, w16_ref[:, col0:col0 + width], preferred_element_type=F32)

    cos = cos_ref[...]
    sin = sin_ref[...]
    lane = lax.broadcasted_iota(jnp.int32, (rows, LANES), 1)
    first_half = (lane & (DIFF_HEAD_DIM // 2)) == 0
    low_head = lane < DIFF_HEAD_DIM

    def rope(t):
        rot = jnp.where(first_half, pltpu.roll(t, LANES - DIFF_HEAD_DIM // 2, 1),
                        pltpu.roll(t, DIFF_HEAD_DIM // 2, 1))
        return t * cos + rot * sin

    vg = _gelu(proj(3 * att_width + gmlp_width, gmlp_width))
    u = _gelu(proj(3 * att_width, gmlp_width))
    n_chunks = rows // GMLP_CHUNK
    gate_rhs = []
    for g in range(N_GMLP_GROUPS):
        gs = slice(g * GMLP_GROUP_DIM, (g + 1) * GMLP_GROUP_DIM)
        vn = _layer_norm(vg[:, gs], lng_ref[:, gs], lnb_ref[:, gs]).astype(BF16)
        gate_rhs.append(jnp.concatenate(
            [vn[c * GMLP_CHUNK:(c + 1) * GMLP_CHUNK] for c in range(n_chunks)], axis=1))

    n_pairs = att_width // PAIR_DIM
    hq = proj(0, att_width)
    for p in range(n_pairs):
        r = rope(hq[:, p * PAIR_DIM:(p + 1) * PAIR_DIM]) * (DIFF_HEAD_DIM ** -0.5 * LOG2_E)
        q_ref[0, 2 * p] = jnp.where(low_head, r, 0.0).astype(BF16)
        q_ref[0, 2 * p + 1] = jnp.where(low_head, 0.0, r).astype(BF16)
    hk = proj(att_width, att_width)
    for p in range(n_pairs):
        k_ref[0, p] = rope(hk[:, p * PAIR_DIM:(p + 1) * PAIR_DIM]).astype(BF16)
    hv = proj(2 * att_width, att_width)
    for p in range(n_pairs):
        vt_ref[0, p, 0, :PAIR_DIM, :] = hv[:, p * PAIR_DIM:(p + 1) * PAIR_DIM].T.astype(BF16)
        extra = lax.broadcasted_iota(jnp.int32, (BF16_SUBLANES, rows), 0)
        vt_ref[0, p, 0, PAIR_DIM:, :] = jnp.where(extra == 0, 1.0, 0.0).astype(BF16)

    t_idx = lax.broadcasted_iota(jnp.int32, (GMLP_CHUNK, GMLP_CHUNK), 0)
    s_idx = lax.broadcasted_iota(jnp.int32, (GMLP_CHUNK, GMLP_CHUNK), 1)
    for g in range(N_GMLP_GROUPS):
        gs = slice(g * GMLP_GROUP_DIM, (g + 1) * GMLP_GROUP_DIM)
        w_causal = jnp.where(t_idx >= s_idx, wsp_ref[g], 0.0).astype(BF16)
        gate = jnp.dot(w_causal, gate_rhs[g], preferred_element_type=F32)
        bias = bsp_ref[:, g:g + 1]
        for c in range(n_chunks):
            rs = slice(c * GMLP_CHUNK, (c + 1) * GMLP_CHUNK)
            gate_c = gate[:, c * GMLP_GROUP_DIM:(c + 1) * GMLP_GROUP_DIM] + bias
            ob_ref[0, rs, gs] = (u[rs, gs] * gate_c).astype(BF16)


def _attn_kernel(q_ref, k_ref, vt_ref, lq1_ref, lk1_ref, lq2_ref, lk2_ref, g_ref,
                 wo_ref, wg_ref, wu_ref, wd_ref,
                 o_ref, wo16_ref, wg16_ref, wu16_ref, wd16_ref,
                 m_ref, acc_ref, sa_ref, sb_ref, sc_ref, ma_ref, mb_ref, mc_ref,
                 *, lambda_init):
    for src, dst in ((wo_ref, wo16_ref), (wg_ref, wg16_ref), (wu_ref, wu16_ref),
                     (wd_ref, wd16_ref)):
        dst[...] = src[...].astype(BF16)

    tk = vt_ref.shape[4]
    tq = tk
    n_q = q_ref.shape[2] // tq
    groups = range(0, 2 * tq, ATT_COLS)

    def live_keys(c0, diagonal):
        return min(tk, (c0 & (tq - 1)) + ATT_COLS) if diagonal else tk

    def scores(qi, kt, s_ref, mx_ref, diagonal):
        q = q_ref[0, :, pl.ds(pl.multiple_of(qi * tq, tq), tq), :].reshape(2 * tq, PAIR_DIM)
        k = k_ref[0, 0, pl.ds(pl.multiple_of(kt * tk, tk), tk), :]
        for g, c0 in enumerate(groups):
            cols = slice(c0, c0 + ATT_COLS)
            n_keys = live_keys(c0, diagonal)
            s = lax.dot_general(k[:n_keys], q[cols], (((1,), (1,)), ((), ())),
                                preferred_element_type=F32)
            if diagonal:
                k_chunk = lax.broadcasted_iota(jnp.int32, s.shape, 0) // CHUNK
                q_pos = lax.broadcasted_iota(jnp.int32, s.shape, 1) + (c0 & (tq - 1))
                s = jnp.where(k_chunk <= q_pos // CHUNK, s, MASKED_SCORE)
            s_ref[g, :n_keys] = s
            mx_ref[:, cols] = jnp.max(s, axis=0, keepdims=True)

    def update(kt, s_ref, mx_ref, diagonal):
        vt = vt_ref[0, 0, kt]
        for g, c0 in enumerate(groups):
            cols = slice(c0, c0 + ATT_COLS)
            n_keys = live_keys(c0, diagonal)
            m_prev = m_ref[:, cols]
            m_new = jnp.maximum(m_prev, mx_ref[:, cols])
            alpha = jnp.exp2(m_prev - m_new)
            p = jnp.exp2(s_ref[g, :n_keys] - m_new)
            acc_ref[g] = alpha * acc_ref[g] + jnp.dot(
                vt[:, :n_keys], p.astype(BF16), preferred_element_type=F32)
            m_ref[:, cols] = m_new

    def start_tile():
        m_ref[...] = jnp.full(m_ref.shape, MASKED_SCORE, F32)
        acc_ref[...] = jnp.zeros(acc_ref.shape, F32)

    lam = (jnp.exp(jnp.sum(lq1_ref[...] * lk1_ref[...], axis=1, keepdims=True))
           - jnp.exp(jnp.sum(lq2_ref[...] * lk2_ref[...], axis=1, keepdims=True))
           + lambda_init)

    def finish_tile(qi):
        acc = jnp.concatenate([acc_ref[g] for g in range(len(groups))], axis=1)
        o_all = acc[:PAIR_DIM] / acc[PAIR_DIM:PAIR_DIM + 1]
        o = o_all[:, :tq] - lam * o_all[:, tq:]
        o = o * lax.rsqrt(jnp.mean(o * o, axis=0, keepdims=True) + LN_EPS)
        o = o.T * (g_ref[...] * (1.0 - lambda_init))
        o_ref[0, pl.ds(pl.multiple_of(qi * tq, tq), tq), :] = o.astype(o_ref.dtype)

    start_tile()
    scores(0, 0, sc_ref, mc_ref, True)
    update(0, sc_ref, mc_ref, True)
    second = min(1, n_q - 1)
    scores(second, second, sc_ref, mc_ref, True)

    def query_tile(i, carry):
        nxt = jnp.minimum(i + 1, n_q - 1)
        finish_tile(i - 1)
        start_tile()
        scores(i, 0, sa_ref, ma_ref, False)
        update(i, sc_ref, mc_ref, True)
        bufs = ((sa_ref, ma_ref), (sb_ref, mb_ref))

        def next_scores_then_update(j):
            scores(i, j.value + 1, *bufs[(j.parity + 1) % 2], False)
            update(j.value, *bufs[j.parity], False)

        def unrolled(t, c):
            for u in range(ATT_UNROLL):
                next_scores_then_update(_Tile(ATT_UNROLL * t + u, u % 2))
            return c

        done = (i - 1) // ATT_UNROLL
        lax.fori_loop(0, done, unrolled, 0)

        base = ATT_UNROLL * done
        for left in range(ATT_UNROLL):
            @pl.when((i - 1) % ATT_UNROLL == left)
            def _(left=left):
                for u in range(left):
                    next_scores_then_update(_Tile(base + u, u % 2))
                scores(nxt, nxt, sc_ref, mc_ref, True)
                update(base + left, *bufs[left % 2], False)

        return carry

    lax.fori_loop(1, n_q, query_tile, 0)
    finish_tile(n_q - 1)


def _out_ffn_kernel(x_ref, oa_ref, ob_ref, wo_ref, g1_ref, b1_ref, wg_ref, wu_ref,
                    cw_ref, cb_ref, wd_ref, g2_ref, b2_ref, out_ref, gate_ref, *, alpha):
    rows = x_ref.shape[1]
    d_ff = wg_ref.shape[1]
    seq_tile = pl.program_id(1)
    halves = [slice(0, rows // 2), slice(rows // 2, rows)]
    whole = [slice(0, rows)]

    def mixed(rs):
        cat = jnp.concatenate([oa_ref[0, rs], ob_ref[0, rs]], axis=1)
        mix = jnp.dot(cat, wo_ref[...], preferred_element_type=F32)
        return _layer_norm(alpha * x_ref[0, rs] + mix, g1_ref[...], b1_ref[...])

    y1 = jnp.concatenate([mixed(rs) for rs in halves], axis=0)
    yb = y1.astype(BF16)

    @pl.when(seq_tile == 0)
    def _():
        gate_ref[:SUBLANES, :] = jnp.zeros((SUBLANES, d_ff), F32)

    def ffn_chunk(rs, cs, f_rows):
        g = jnp.dot(yb[rs], wg_ref[:, cs], preferred_element_type=F32)
        up = jnp.dot(yb[rs], wu_ref[:, cs], preferred_element_type=F32)
        gate_ref[SUBLANES + rs.start:SUBLANES + rs.stop, cs] = g
        conv = cb_ref[:, cs] + cw_ref[CONV_WIDTH - 1:CONV_WIDTH, cs] * g
        for tap in range(CONV_WIDTH - 1):
            back = CONV_WIDTH - 1 - tap
            window = slice(SUBLANES + rs.start - back, SUBLANES + rs.stop - back)
            conv = conv + cw_ref[tap:tap + 1, cs] * gate_ref[window, cs]
        half = 0.5 * conv
        h = ((half + half * jnp.tanh(half)) * up).astype(BF16)
        return f_rows + jnp.dot(h, wd_ref[cs, :], preferred_element_type=F32)

    f = jnp.zeros((rows, x_ref.shape[2]), F32)
    starts = list(range(0, d_ff, FFN_COLS))
    for c0 in starts:
        cs = slice(c0, min(c0 + FFN_COLS, d_ff))
        parts = halves if c0 in (starts[0], starts[-1]) else whole
        f = jnp.concatenate([ffn_chunk(rs, cs, f[rs]) for rs in parts], axis=0)
        gate_ref[:SUBLANES, cs] = gate_ref[rows:rows + SUBLANES, cs]
    for rs in halves:
        out_ref[0, rs] = _layer_norm(alpha * y1[rs] + f[rs], g2_ref[...], b2_ref[...])


def _rope_tables(seq_len):
    pos = jnp.arange(seq_len, dtype=F32)
    inv_freq = 1.0 / (ROPE_THETA ** (jnp.arange(0, DIFF_HEAD_DIM, 2, dtype=F32) / DIFF_HEAD_DIM))
    ang = pos[:, None] * inv_freq[None, :]
    cos, sin = jnp.cos(ang), jnp.sin(ang)
    reps = LANES // (DIFF_HEAD_DIM // 2)
    cos_t = jnp.tile(cos, (1, reps))
    sin_t = jnp.tile(jnp.concatenate([-sin, sin], axis=1), (1, reps // 2))
    return cos_t, sin_t


def _const_spec(shape):
    return pl.BlockSpec(shape, lambda *_: (0,) * len(shape), pipeline_mode=pl.Buffered(1))


def _layer(x, cos_t, sin_t, lambda_init, alpha, w_in, lq1, lk1, lq2, lk2, subln_g, lng, lnb,
           w_sp, b_sp, w_out, ln1_g, ln1_b, w_gate, w_up, conv_w, conv_b, w_down, ln2_g, ln2_b):
    B, S, D = x.shape
    att_width = N_DIFF_HEADS * PAIR_DIM
    gmlp_width = N_GMLP_GROUPS * GMLP_GROUP_DIM
    d_ff = w_gate.shape[1]
    n_pairs = N_DIFF_HEADS
    n_groups = 2 * IN_ROWS // ATT_COLS
    row = lambda a: a.reshape(1, -1)

    q, k, v, ob = pl.pallas_call(
        functools.partial(_in_proj_kernel, att_width=att_width, gmlp_width=gmlp_width),
        grid=(B, S // IN_ROWS),
        in_specs=[
            pl.BlockSpec((1, IN_ROWS, D), lambda b, i: (b, i, 0)),
            _const_spec(w_in.shape),
            pl.BlockSpec((IN_ROWS, LANES), lambda b, i: (i, 0)),
            pl.BlockSpec((IN_ROWS, LANES), lambda b, i: (i, 0)),
            _const_spec((1, gmlp_width)),
            _const_spec((1, gmlp_width)),
            _const_spec(w_sp.shape),
            _const_spec((GMLP_CHUNK, N_GMLP_GROUPS)),
        ],
        out_specs=[
            pl.BlockSpec((1, 2 * n_pairs, IN_ROWS, PAIR_DIM), lambda b, i: (b, 0, i, 0)),
            pl.BlockSpec((1, n_pairs, IN_ROWS, PAIR_DIM), lambda b, i: (b, 0, i, 0)),
            pl.BlockSpec((1, n_pairs, 1, VT_ROWS, IN_ROWS), lambda b, i: (b, 0, i, 0, 0)),
            pl.BlockSpec((1, IN_ROWS, gmlp_width), lambda b, i: (b, i, 0)),
        ],
        out_shape=[
            jax.ShapeDtypeStruct((B, 2 * n_pairs, S, PAIR_DIM), BF16),
            jax.ShapeDtypeStruct((B, n_pairs, S, PAIR_DIM), BF16),
            jax.ShapeDtypeStruct((B, n_pairs, S // IN_ROWS, VT_ROWS, IN_ROWS), BF16),
            jax.ShapeDtypeStruct((B, S, gmlp_width), BF16),
        ],
        scratch_shapes=[pltpu.VMEM(w_in.shape, BF16)],
        compiler_params=pltpu.CompilerParams(
            dimension_semantics=("arbitrary", "arbitrary"), vmem_limit_bytes=VMEM_LIMIT),
        name="in_proj_gmlp",
    )(x, w_in, cos_t, sin_t, row(lng), row(lnb), w_sp, b_sp.T)

    n_steps = B * n_pairs

    def slab_spec(w):
        slab = next(s for s in range(BF16_SUBLANES, w.shape[0] + 1, BF16_SUBLANES)
                    if w.shape[0] % s == 0 and w.shape[0] // s <= n_steps)
        last = w.shape[0] // slab - 1
        return pl.BlockSpec((slab, w.shape[1]),
                            lambda b, p: (jnp.minimum(b * n_pairs + p, last), 0))

    ffn_weights = (w_out, w_gate, w_up, w_down)
    oa, w_out16, w_gate16, w_up16, w_down16 = pl.pallas_call(
        functools.partial(_attn_kernel, lambda_init=lambda_init),
        grid=(B, n_pairs),
        in_specs=[
            pl.BlockSpec((1, 2, S, PAIR_DIM), lambda b, p: (b, p, 0, 0)),
            pl.BlockSpec((1, 1, S, PAIR_DIM), lambda b, p: (b, p, 0, 0)),
            pl.BlockSpec((1, 1, S // IN_ROWS, VT_ROWS, IN_ROWS), lambda b, p: (b, p, 0, 0, 0)),
            _const_spec((1, DIFF_HEAD_DIM)),
            _const_spec((1, DIFF_HEAD_DIM)),
            _const_spec((1, DIFF_HEAD_DIM)),
            _const_spec((1, DIFF_HEAD_DIM)),
            _const_spec((1, PAIR_DIM)),
        ] + [slab_spec(w) for w in ffn_weights],
        out_specs=[pl.BlockSpec((1, S, PAIR_DIM), lambda b, p: (b, 0, p))]
        + [slab_spec(w) for w in ffn_weights],
        out_shape=[jax.ShapeDtypeStruct((B, S, att_width), BF16)]
        + [jax.ShapeDtypeStruct(w.shape, BF16) for w in ffn_weights],
        scratch_shapes=[
            pltpu.VMEM((1, 2 * IN_ROWS), F32),
            pltpu.VMEM((n_groups, VT_ROWS, ATT_COLS), F32),
            pltpu.VMEM((n_groups, IN_ROWS, ATT_COLS), F32),
            pltpu.VMEM((n_groups, IN_ROWS, ATT_COLS), F32),
            pltpu.VMEM((n_groups, IN_ROWS, ATT_COLS), F32),
            pltpu.VMEM((1, 2 * IN_ROWS), F32),
            pltpu.VMEM((1, 2 * IN_ROWS), F32),
            pltpu.VMEM((1, 2 * IN_ROWS), F32),
        ],
        compiler_params=pltpu.CompilerParams(
            dimension_semantics=("arbitrary", "arbitrary"),
            vmem_limit_bytes=VMEM_LIMIT),
        name="diff_attention",
    )(q, k, v, row(lq1), row(lk1), row(lq2), row(lk2), row(subln_g), *ffn_weights)

    out = pl.pallas_call(
        functools.partial(_out_ffn_kernel, alpha=alpha),
        grid=(B, S // FFN_ROWS),
        in_specs=[
            pl.BlockSpec((1, FFN_ROWS, D), lambda b, i: (b, i, 0)),
            pl.BlockSpec((1, FFN_ROWS, att_width), lambda b, i: (b, i, 0)),
            pl.BlockSpec((1, FFN_ROWS, gmlp_width), lambda b, i: (b, i, 0)),
            _const_spec(w_out.shape),
            _const_spec((1, D)),
            _const_spec((1, D)),
            _const_spec(w_gate.shape),
            _const_spec(w_up.shape),
            _const_spec(conv_w.shape),
            _const_spec((1, d_ff)),
            _const_spec(w_down.shape),
            _const_spec((1, D)),
            _const_spec((1, D)),
        ],
        out_specs=pl.BlockSpec((1, FFN_ROWS, D), lambda b, i: (b, i, 0)),
        out_shape=jax.ShapeDtypeStruct((B, S, D), x.dtype),
        scratch_shapes=[pltpu.VMEM((SUBLANES + FFN_ROWS, d_ff), F32)],
        compiler_params=pltpu.CompilerParams(
            dimension_semantics=("arbitrary", "arbitrary"), vmem_limit_bytes=VMEM_LIMIT),
        name="out_proj_ffn",
    )(x, oa, ob, w_out16, row(ln1_g), row(ln1_b), w_gate16, w_up16, conv_w, row(conv_b),
      w_down16, row(ln2_g), row(ln2_b))
    return out


def kernel(x, w_in, lambda_q1, lambda_k1, lambda_q2, lambda_k2, subln_g, gmlp_ln_g, gmlp_ln_b,
           w_spatial, b_spatial, w_out, ln1_g, ln1_b, w_gate, w_up, conv_w, conv_b, w_down,
           ln2_g, ln2_b):
    depth = w_in.shape[0]
    alpha = (2 * depth) ** 0.25
    cos_t, sin_t = _rope_tables(x.shape[1])
    for l in range(depth):
        lambda_init = 0.8 - 0.6 * math.exp(-0.3 * l)
        x = _layer(x, cos_t, sin_t, lambda_init, alpha, w_in[l], lambda_q1[l], lambda_k1[l],
                   lambda_q2[l], lambda_k2[l], subln_g[l], gmlp_ln_g[l], gmlp_ln_b[l],
                   w_spatial[l], b_spatial[l], w_out[l], ln1_g[l], ln1_b[l], w_gate[l],
                   w_up[l], conv_w[l], conv_b[l], w_down[l], ln2_g[l], ln2_b[l])
    return x
```

```python
import functools
import math
from typing import Any, NamedTuple

import jax
import jax.numpy as jnp
from jax import lax
from jax.experimental import pallas as pl
from jax.experimental.pallas import tpu as pltpu

F32 = jnp.float32
BF16 = jnp.bfloat16

CHUNK = 64
N_DIFF_HEADS = 4
DIFF_HEAD_DIM = 64
PAIR_DIM = 2 * DIFF_HEAD_DIM
N_GMLP_GROUPS = 4
GMLP_GROUP_DIM = 128
GMLP_CHUNK = 128
CONV_WIDTH = 3
ROPE_THETA = 10000.0
LN_EPS = 1e-5

LANES = 128
SUBLANES = 8
BF16_SUBLANES = 16
VT_ROWS = PAIR_DIM + BF16_SUBLANES
LOG2_E = math.log2(math.e)

IN_ROWS = 512
ATT_COLS = 256
ATT_UNROLL = 4
FFN_ROWS = 512
FFN_FIRST_COLS = 1024
FFN_LAST_COLS = 768
MASKED_SCORE = -1e30
VMEM_LIMIT = 56 * 1024 * 1024


class _Tile(NamedTuple):
    value: Any
    parity: int


def _layer_norm(z, g, b):
    mu = jnp.mean(z, axis=-1, keepdims=True)
    d = z - mu
    var = jnp.mean(d * d, axis=-1, keepdims=True)
    return d * lax.rsqrt(var + LN_EPS) * g + b


def _gelu(z):
    return 0.5 * z * (1.0 + lax.erf(z * (1.0 / math.sqrt(2.0))))


def _in_proj_kernel(x_ref, w_ref, cos_ref, sin_ref, lng_ref, lnb_ref, wsp_ref, bsp_ref,
                    q_ref, k_ref, vt_ref, ob_ref, w16_ref, *, att_width, gmlp_width):
    rows = x_ref.shape[1]
    xb = x_ref[0].astype(BF16)

    @pl.when((pl.program_id(0) == 0) & (pl.program_id(1) == 0))
    def _():
        w16_ref[...] = w_ref[...].astype(BF16)

    def proj(col0, width):
        return jnp.dot(xb, w16_ref[:, col0:col0 + width], preferred_element_type=F32)

    cos = cos_ref[...]
    sin = sin_ref[...]
    lane = lax.broadcasted_iota(jnp.int32, (rows, LANES), 1)
    first_half = (lane & (DIFF_HEAD_DIM // 2)) == 0
    low_head = lane < DIFF_HEAD_DIM

    def rope(t):
        rot = jnp.where(first_half, pltpu.roll(t, LANES - DIFF_HEAD_DIM // 2, 1),
                        pltpu.roll(t, DIFF_HEAD_DIM // 2, 1))
        return t * cos + rot * sin

    vg = _gelu(proj(3 * att_width + gmlp_width, gmlp_width))
    u = _gelu(proj(3 * att_width, gmlp_width))
    n_chunks = rows // GMLP_CHUNK
    gate_rhs = []
    for g in range(N_GMLP_GROUPS):
        gs = slice(g * GMLP_GROUP_DIM, (g + 1) * GMLP_GROUP_DIM)
        vn = _layer_norm(vg[:, gs], lng_ref[:, gs], lnb_ref[:, gs]).astype(BF16)
        gate_rhs.append(jnp.concatenate(
            [vn[c * GMLP_CHUNK:(c + 1) * GMLP_CHUNK] for c in range(n_chunks)], axis=1))

    n_pairs = att_width // PAIR_DIM
    hq = proj(0, att_width)
    for p in range(n_pairs):
        r = rope(hq[:, p * PAIR_DIM:(p + 1) * PAIR_DIM]) * (DIFF_HEAD_DIM ** -0.5 * LOG2_E)
        q_ref[0, 2 * p] = jnp.where(low_head, r, 0.0).astype(BF16)
        q_ref[0, 2 * p + 1] = jnp.where(low_head, 0.0, r).astype(BF16)
    hk = proj(att_width, att_width)
    for p in range(n_pairs):
        k_ref[0, p] = rope(hk[:, p * PAIR_DIM:(p + 1) * PAIR_DIM]).astype(BF16)
    hv = proj(2 * att_width, att_width)
    for p in range(n_pairs):
        vt_ref[0, p, 0, :PAIR_DIM, :] = hv[:, p * PAIR_DIM:(p + 1) * PAIR_DIM].T.astype(BF16)
        extra = lax.broadcasted_iota(jnp.int32, (BF16_SUBLANES, rows), 0)
        vt_ref[0, p, 0, PAIR_DIM:, :] = jnp.where(extra == 0, 1.0, 0.0).astype(BF16)

    t_idx = lax.broadcasted_iota(jnp.int32, (GMLP_CHUNK, GMLP_CHUNK), 0)
    s_idx = lax.broadcasted_iota(jnp.int32, (GMLP_CHUNK, GMLP_CHUNK), 1)
    for g in range(N_GMLP_GROUPS):
        gs = slice(g * GMLP_GROUP_DIM, (g + 1) * GMLP_GROUP_DIM)
        w_causal = jnp.where(t_idx >= s_idx, wsp_ref[g], 0.0).astype(BF16)
        gate = jnp.dot(w_causal, gate_rhs[g], preferred_element_type=F32)
        bias = bsp_ref[:, g:g + 1]
        for c in range(n_chunks):
            rs = slice(c * GMLP_CHUNK, (c + 1) * GMLP_CHUNK)
            gate_c = gate[:, c * GMLP_GROUP_DIM:(c + 1) * GMLP_GROUP_DIM] + bias
            ob_ref[0, rs, gs] = (u[rs, gs] * gate_c).astype(BF16)


def _attn_kernel(q_ref, k_ref, vt_ref, lq1_ref, lk1_ref, lq2_ref, lk2_ref, g_ref,
                 wo_ref, wg_ref, wu_ref, wd_ref,
                 o_ref, wo16_ref, wg16_ref, wu16_ref, wd16_ref,
                 m_ref, acc_ref, sa_ref, sb_ref, sc_ref, ma_ref, mb_ref, mc_ref,
                 *, lambda_init):
    for src, dst in ((wo_ref, wo16_ref), (wg_ref, wg16_ref), (wu_ref, wu16_ref),
                     (wd_ref, wd16_ref)):
        dst[...] = src[...].astype(BF16)

    tk = vt_ref.shape[4]
    tq = tk
    n_q = q_ref.shape[2] // tq
    groups = range(0, 2 * tq, ATT_COLS)

    def live_keys(c0, diagonal):
        return min(tk, (c0 & (tq - 1)) + ATT_COLS) if diagonal else tk

    def scores(qi, kt, s_ref, mx_ref, diagonal):
        q = q_ref[0, :, pl.ds(pl.multiple_of(qi * tq, tq), tq), :].reshape(2 * tq, PAIR_DIM)
        k = k_ref[0, 0, pl.ds(pl.multiple_of(kt * tk, tk), tk), :]
        for g, c0 in enumerate(groups):
            cols = slice(c0, c0 + ATT_COLS)
            n_keys = live_keys(c0, diagonal)
            s = lax.dot_general(k[:n_keys], q[cols], (((1,), (1,)), ((), ())),
                                preferred_element_type=F32)
            if diagonal:
                k_chunk = lax.broadcasted_iota(jnp.int32, s.shape, 0) // CHUNK
                q_pos = lax.broadcasted_iota(jnp.int32, s.shape, 1) + (c0 & (tq - 1))
                s = jnp.where(k_chunk <= q_pos // CHUNK, s, MASKED_SCORE)
            s_ref[g, :n_keys] = s
            mx_ref[:, cols] = jnp.max(s, axis=0, keepdims=True)

    def update(kt, s_ref, mx_ref, diagonal):
        vt = vt_ref[0, 0, kt]
        for g, c0 in enumerate(groups):
            cols = slice(c0, c0 + ATT_COLS)
            n_keys = live_keys(c0, diagonal)
            m_prev = m_ref[:, cols]
            m_new = jnp.maximum(m_prev, mx_ref[:, cols])
            alpha = jnp.exp2(m_prev - m_new)
            p = jnp.exp2(s_ref[g, :n_keys] - m_new)
            acc_ref[g] = alpha * acc_ref[g] + jnp.dot(
                vt[:, :n_keys], p.astype(BF16), preferred_element_type=F32)
            m_ref[:, cols] = m_new

    def start_tile():
        m_ref[...] = jnp.full(m_ref.shape, MASKED_SCORE, F32)
        acc_ref[...] = jnp.zeros(acc_ref.shape, F32)

    lam = (jnp.exp(jnp.sum(lq1_ref[...] * lk1_ref[...], axis=1, keepdims=True))
           - jnp.exp(jnp.sum(lq2_ref[...] * lk2_ref[...], axis=1, keepdims=True))
           + lambda_init)

    def finish_tile(qi):
        acc = jnp.concatenate([acc_ref[g] for g in range(len(groups))], axis=1)
        o_all = acc[:PAIR_DIM] / acc[PAIR_DIM:PAIR_DIM + 1]
        o = o_all[:, :tq] - lam * o_all[:, tq:]
        o = o * lax.rsqrt(jnp.mean(o * o, axis=0, keepdims=True) + LN_EPS)
        o = o.T * (g_ref[...] * (1.0 - lambda_init))
        o_ref[0, pl.ds(pl.multiple_of(qi * tq, tq), tq), :] = o.astype(o_ref.dtype)

    start_tile()
    scores(0, 0, sc_ref, mc_ref, True)
    update(0, sc_ref, mc_ref, True)
    second = min(1, n_q - 1)
    scores(second, second, sc_ref, mc_ref, True)

    def query_tile(i, carry):
        nxt = jnp.minimum(i + 1, n_q - 1)
        finish_tile(i - 1)
        start_tile()
        scores(i, 0, sa_ref, ma_ref, False)
        update(i, sc_ref, mc_ref, True)
        bufs = ((sa_ref, ma_ref), (sb_ref, mb_ref))

        def next_scores_then_update(j):
            scores(i, j.value + 1, *bufs[(j.parity + 1) % 2], False)
            update(j.value, *bufs[j.parity], False)

        def unrolled(t, c):
            for u in range(ATT_UNROLL):
                next_scores_then_update(_Tile(ATT_UNROLL * t + u, u % 2))
            return c

        done = (i - 1) // ATT_UNROLL
        lax.fori_loop(0, done, unrolled, 0)

        base = ATT_UNROLL * done
        for left in range(ATT_UNROLL):
            @pl.when((i - 1) % ATT_UNROLL == left)
            def _(left=left):
                for u in range(left):
                    next_scores_then_update(_Tile(base + u, u % 2))
                scores(nxt, nxt, sc_ref, mc_ref, True)
                update(base + left, *bufs[left % 2], False)

        return carry

    lax.fori_loop(1, n_q, query_tile, 0)
    finish_tile(n_q - 1)


def _out_ffn_kernel(x_ref, oa_ref, ob_ref, wo_ref, g1_ref, b1_ref, wg_ref, wu_ref,
                    cw_ref, cb_ref, wd_ref, g2_ref, b2_ref, out_ref, gate_ref, *, alpha):
    rows = x_ref.shape[1]
    d_ff = wg_ref.shape[1]
    seq_tile = pl.program_id(1)
    halves = [slice(0, rows // 2), slice(rows // 2, rows)]
    whole = [slice(0, rows)]

    def mixed(rs):
        cat = jnp.concatenate([oa_ref[0, rs], ob_ref[0, rs]], axis=1)
        mix = jnp.dot(cat, wo_ref[...], preferred_element_type=F32)
        return _layer_norm(alpha * x_ref[0, rs] + mix, g1_ref[...], b1_ref[...])

    y1 = jnp.concatenate([mixed(rs) for rs in halves], axis=0)
    yb = y1.astype(BF16)

    @pl.when(seq_tile == 0)
    def _():
        gate_ref[:SUBLANES, :] = jnp.zeros((SUBLANES, d_ff), F32)

    def ffn_chunk(rs, cs, f_rows):
        g = jnp.dot(yb[rs], wg_ref[:, cs], preferred_element_type=F32)
        up = jnp.dot(yb[rs], wu_ref[:, cs], preferred_element_type=F32)
        gate_ref[SUBLANES + rs.start:SUBLANES + rs.stop, cs] = g
        conv = cb_ref[:, cs] + cw_ref[CONV_WIDTH - 1:CONV_WIDTH, cs] * g
        for tap in range(CONV_WIDTH - 1):
            back = CONV_WIDTH - 1 - tap
            window = slice(SUBLANES + rs.start - back, SUBLANES + rs.stop - back)
            conv = conv + cw_ref[tap:tap + 1, cs] * gate_ref[window, cs]
        half = 0.5 * conv
        h = ((half + half * jnp.tanh(half)) * up).astype(BF16)
        return f_rows + jnp.dot(h, wd_ref[cs, :], preferred_element_type=F32)

    f = jnp.zeros((rows, x_ref.shape[2]), F32)
    bounds = [0, FFN_FIRST_COLS, d_ff - FFN_LAST_COLS, d_ff]
    for c0, c1 in zip(bounds[:-1], bounds[1:]):
        cs = slice(c0, c1)
        parts = halves if c0 == 0 or c1 == d_ff else whole
        f = jnp.concatenate([ffn_chunk(rs, cs, f[rs]) for rs in parts], axis=0)
        gate_ref[:SUBLANES, cs] = gate_ref[rows:rows + SUBLANES, cs]
    for rs in halves:
        out_ref[0, rs] = _layer_norm(alpha * y1[rs] + f[rs], g2_ref[...], b2_ref[...])


def _rope_tables(seq_len):
    pos = jnp.arange(seq_len, dtype=F32)
    inv_freq = 1.0 / (ROPE_THETA ** (jnp.arange(0, DIFF_HEAD_DIM, 2, dtype=F32) / DIFF_HEAD_DIM))
    ang = pos[:, None] * inv_freq[None, :]
    cos, sin = jnp.cos(ang), jnp.sin(ang)
    reps = LANES // (DIFF_HEAD_DIM // 2)
    cos_t = jnp.tile(cos, (1, reps))
    sin_t = jnp.tile(jnp.concatenate([-sin, sin], axis=1), (1, reps // 2))
    return cos_t, sin_t


def _const_spec(shape):
    return pl.BlockSpec(shape, lambda *_: (0,) * len(shape), pipeline_mode=pl.Buffered(1))


def _layer(x, cos_t, sin_t, lambda_init, alpha, w_in, lq1, lk1, lq2, lk2, subln_g, lng, lnb,
           w_sp, b_sp, w_out, ln1_g, ln1_b, w_gate, w_up, conv_w, conv_b, w_down, ln2_g, ln2_b):
    B, S, D = x.shape
    att_width = N_DIFF_HEADS * PAIR_DIM
    gmlp_width = N_GMLP_GROUPS * GMLP_GROUP_DIM
    d_ff = w_gate.shape[1]
    n_pairs = N_DIFF_HEADS
    n_groups = 2 * IN_ROWS // ATT_COLS
    row = lambda a: a.reshape(1, -1)

    q, k, v, ob = pl.pallas_call(
        functools.partial(_in_proj_kernel, att_width=att_width, gmlp_width=gmlp_width),
        grid=(B, S // IN_ROWS),
        in_specs=[
            pl.BlockSpec((1, IN_ROWS, D), lambda b, i: (b, i, 0)),
            _const_spec(w_in.shape),
            pl.BlockSpec((IN_ROWS, LANES), lambda b, i: (i, 0)),
            pl.BlockSpec((IN_ROWS, LANES), lambda b, i: (i, 0)),
            _const_spec((1, gmlp_width)),
            _const_spec((1, gmlp_width)),
            _const_spec(w_sp.shape),
            _const_spec((GMLP_CHUNK, N_GMLP_GROUPS)),
        ],
        out_specs=[
            pl.BlockSpec((1, 2 * n_pairs, IN_ROWS, PAIR_DIM), lambda b, i: (b, 0, i, 0)),
            pl.BlockSpec((1, n_pairs, IN_ROWS, PAIR_DIM), lambda b, i: (b, 0, i, 0)),
            pl.BlockSpec((1, n_pairs, 1, VT_ROWS, IN_ROWS), lambda b, i: (b, 0, i, 0, 0)),
            pl.BlockSpec((1, IN_ROWS, gmlp_width), lambda b, i: (b, i, 0)),
        ],
        out_shape=[
            jax.ShapeDtypeStruct((B, 2 * n_pairs, S, PAIR_DIM), BF16),
            jax.ShapeDtypeStruct((B, n_pairs, S, PAIR_DIM), BF16),
            jax.ShapeDtypeStruct((B, n_pairs, S // IN_ROWS, VT_ROWS, IN_ROWS), BF16),
            jax.ShapeDtypeStruct((B, S, gmlp_width), BF16),
        ],
        scratch_shapes=[pltpu.VMEM(w_in.shape, BF16)],
        compiler_params=pltpu.CompilerParams(
            dimension_semantics=("arbitrary", "arbitrary"), vmem_limit_bytes=VMEM_LIMIT),
        name="in_proj_gmlp",
    )(x, w_in, cos_t, sin_t, row(lng), row(lnb), w_sp, b_sp.T)

    n_steps = B * n_pairs

    def slab_spec(w):
        slab = next(s for s in range(BF16_SUBLANES, w.shape[0] + 1, BF16_SUBLANES)
                    if w.shape[0] % s == 0 and w.shape[0] // s <= n_steps)
        last = w.shape[0] // slab - 1
        return pl.BlockSpec((slab, w.shape[1]),
                            lambda b, p: (jnp.minimum(b * n_pairs + p, last), 0))

    ffn_weights = (w_out, w_gate, w_up, w_down)
    oa, w_out16, w_gate16, w_up16, w_down16 = pl.pallas_call(
        functools.partial(_attn_kernel, lambda_init=lambda_init),
        grid=(B, n_pairs),
        in_specs=[
            pl.BlockSpec((1, 2, S, PAIR_DIM), lambda b, p: (b, p, 0, 0)),
            pl.BlockSpec((1, 1, S, PAIR_DIM), lambda b, p: (b, p, 0, 0)),
            pl.BlockSpec((1, 1, S // IN_ROWS, VT_ROWS, IN_ROWS), lambda b, p: (b, p, 0, 0, 0)),
            _const_spec((1, DIFF_HEAD_DIM)),
            _const_spec((1, DIFF_HEAD_DIM)),
            _const_spec((1, DIFF_HEAD_DIM)),
            _const_spec((1, DIFF_HEAD_DIM)),
            _const_spec((1, PAIR_DIM)),
        ] + [slab_spec(w) for w in ffn_weights],
        out_specs=[pl.BlockSpec((1, S, PAIR_DIM), lambda b, p: (b, 0, p))]
        + [slab_spec(w) for w in ffn_weights],
        out_shape=[jax.ShapeDtypeStruct((B, S, att_width), BF16)]
        + [jax.ShapeDtypeStruct(w.shape, BF16) for w in ffn_weights],
        scratch_shapes=[
            pltpu.VMEM((1, 2 * IN_ROWS), F32),
            pltpu.VMEM((n_groups, VT_ROWS, ATT_COLS), F32),
            pltpu.VMEM((n_groups, IN_ROWS, ATT_COLS), F32),
            pltpu.VMEM((n_groups, IN_ROWS, ATT_COLS), F32),
            pltpu.VMEM((n_groups, IN_ROWS, ATT_COLS), F32),
            pltpu.VMEM((1, 2 * IN_ROWS), F32),
            pltpu.VMEM((1, 2 * IN_ROWS), F32),
            pltpu.VMEM((1, 2 * IN_ROWS), F32),
        ],
        compiler_params=pltpu.CompilerParams(
            dimension_semantics=("arbitrary", "arbitrary"),
            vmem_limit_bytes=VMEM_LIMIT),
        name="diff_attention",
    )(q, k, v, row(lq1), row(lk1), row(lq2), row(lk2), row(subln_g), *ffn_weights)

    out = pl.pallas_call(
        functools.partial(_out_ffn_kernel, alpha=alpha),
        grid=(B, S // FFN_ROWS),
        in_specs=[
            pl.BlockSpec((1, FFN_ROWS, D), lambda b, i: (b, i, 0)),
            pl.BlockSpec((1, FFN_ROWS, att_width), lambda b, i: (b, i, 0)),
            pl.BlockSpec((1, FFN_ROWS, gmlp_width), lambda b, i: (b, i, 0)),
            _const_spec(w_out.shape),
            _const_spec((1, D)),
            _const_spec((1, D)),
            _const_spec(w_gate.shape),
            _const_spec(w_up.shape),
            _const_spec(conv_w.shape),
            _const_spec((1, d_ff)),
            _const_spec(w_down.shape),
            _const_spec((1, D)),
            _const_spec((1, D)),
        ],
        out_specs=pl.BlockSpec((1, FFN_ROWS, D), lambda b, i: (b, i, 0)),
        out_shape=jax.ShapeDtypeStruct((B, S, D), x.dtype),
        scratch_shapes=[pltpu.VMEM((SUBLANES + FFN_ROWS, d_ff), F32)],
        compiler_params=pltpu.CompilerParams(
            dimension_semantics=("arbitrary", "arbitrary"), vmem_limit_bytes=VMEM_LIMIT),
        name="out_proj_ffn",
    )(x, oa, ob, w_out16, row(ln1_g), row(ln1_b), w_gate16, w_up16, conv_w, row(conv_b),
      w_down16, row(ln2_g), row(ln2_b))
    return out


def kernel(x, w_in, lambda_q1, lambda_k1, lambda_q2, lambda_k2, subln_g, gmlp_ln_g, gmlp_ln_b,
           w_spatial, b_spatial, w_out, ln1_g, ln1_b, w_gate, w_up, conv_w, conv_b, w_down,
           ln2_g, ln2_b):
    depth = w_in.shape[0]
    alpha = (2 * depth) ** 0.25
    cos_t, sin_t = _rope_tables(x.shape[1])
    for l in range(depth):
        lambda_init = 0.8 - 0.6 * math.exp(-0.3 * l)
        x = _layer(x, cos_t, sin_t, lambda_init, alpha, w_in[l], lambda_q1[l], lambda_k1[l],
                   lambda_q2[l], lambda_k2[l], subln_g[l], gmlp_ln_g[l], gmlp_ln_b[l],
                   w_spatial[l], b_spatial[l], w_out[l], ln1_g[l], ln1_b[l], w_gate[l],
                   w_up[l], conv_w[l], conv_b[l], w_down[l], ln2_g[l], ln2_b[l])
    return x
```

```python
import functools
import math
from typing import Any, NamedTuple

import jax
import jax.numpy as jnp
from jax import lax
from jax.experimental import pallas as pl
from jax.experimental.pallas import tpu as pltpu

F32 = jnp.float32
BF16 = jnp.bfloat16

CHUNK = 64
N_DIFF_HEADS = 4
DIFF_HEAD_DIM = 64
PAIR_DIM = 2 * DIFF_HEAD_DIM
N_GMLP_GROUPS = 4
GMLP_GROUP_DIM = 128
GMLP_CHUNK = 128
CONV_WIDTH = 3
ROPE_THETA = 10000.0
LN_EPS = 1e-5

LANES = 128
SUBLANES = 8
BF16_SUBLANES = 16
VT_ROWS = PAIR_DIM + BF16_SUBLANES
LOG2_E = math.log2(math.e)

IN_ROWS = 512
ATT_COLS = 256
ATT_UNROLL = 6
FFN_ROWS = 512
FFN_FIRST_COLS = 1024
FFN_LAST_COLS = 768
MASKED_SCORE = -1e30
VMEM_LIMIT = 56 * 1024 * 1024


class _Tile(NamedTuple):
    value: Any
    parity: int


def _layer_norm(z, g, b):
    mu = jnp.mean(z, axis=-1, keepdims=True)
    d = z - mu
    var = jnp.mean(d * d, axis=-1, keepdims=True)
    return d * lax.rsqrt(var + LN_EPS) * g + b


def _gelu(z):
    return 0.5 * z * (1.0 + lax.erf(z * (1.0 / math.sqrt(2.0))))


def _in_proj_kernel(x_ref, w_ref, cos_ref, sin_ref, lng_ref, lnb_ref, wsp_ref, bsp_ref,
                    q_ref, k_ref, vt_ref, ob_ref, w16_ref, *, att_width, gmlp_width):
    @pl.when((pl.program_id(0) == 0) & (pl.program_id(1) == 0))
    def _():
        w16_ref[...] = w_ref[...].astype(BF16)

    rows = x_ref.shape[1]
    xb = x_ref[0].astype(BF16)

    def proj(col0, width):
        return jnp.dot(xb, w16_ref[:, col0:col0 + width], preferred_element_type=F32)

    cos = cos_ref[...]
    sin = sin_ref[...]
    lane = lax.broadcasted_iota(jnp.int32, (rows, LANES), 1)
    first_half = (lane & (DIFF_HEAD_DIM // 2)) == 0
    low_head = lane < DIFF_HEAD_DIM

    def rope(t):
        rot = jnp.where(first_half, pltpu.roll(t, LANES - DIFF_HEAD_DIM // 2, 1),
                        pltpu.roll(t, DIFF_HEAD_DIM // 2, 1))
        return t * cos + rot * sin

    vg = _gelu(proj(3 * att_width + gmlp_width, gmlp_width))
    u = _gelu(proj(3 * att_width, gmlp_width))
    n_chunks = rows // GMLP_CHUNK
    gate_rhs = []
    for g in range(N_GMLP_GROUPS):
        gs = slice(g * GMLP_GROUP_DIM, (g + 1) * GMLP_GROUP_DIM)
        vn = _layer_norm(vg[:, gs], lng_ref[:, gs], lnb_ref[:, gs]).astype(BF16)
        gate_rhs.append(jnp.concatenate(
            [vn[c * GMLP_CHUNK:(c + 1) * GMLP_CHUNK] for c in range(n_chunks)], axis=1))

    n_pairs = att_width // PAIR_DIM
    hq = proj(0, att_width)
    for p in range(n_pairs):
        r = rope(hq[:, p * PAIR_DIM:(p + 1) * PAIR_DIM]) * (DIFF_HEAD_DIM ** -0.5 * LOG2_E)
        q_ref[0, 2 * p] = jnp.where(low_head, r, 0.0).astype(BF16)
        q_ref[0, 2 * p + 1] = jnp.where(low_head, 0.0, r).astype(BF16)
    hk = proj(att_width, att_width)
    for p in range(n_pairs):
        k_ref[0, p] = rope(hk[:, p * PAIR_DIM:(p + 1) * PAIR_DIM]).astype(BF16)
    hv = proj(2 * att_width, att_width)
    for p in range(n_pairs):
        vt_ref[0, p, 0, :PAIR_DIM, :] = hv[:, p * PAIR_DIM:(p + 1) * PAIR_DIM].T.astype(BF16)
        extra = lax.broadcasted_iota(jnp.int32, (BF16_SUBLANES, rows), 0)
        vt_ref[0, p, 0, PAIR_DIM:, :] = jnp.where(extra == 0, 1.0, 0.0).astype(BF16)

    t_idx = lax.broadcasted_iota(jnp.int32, (GMLP_CHUNK, GMLP_CHUNK), 0)
    s_idx = lax.broadcasted_iota(jnp.int32, (GMLP_CHUNK, GMLP_CHUNK), 1)
    for g in range(N_GMLP_GROUPS):
        gs = slice(g * GMLP_GROUP_DIM, (g + 1) * GMLP_GROUP_DIM)
        w_causal = jnp.where(t_idx >= s_idx, wsp_ref[g], 0.0).astype(BF16)
        gate = jnp.dot(w_causal, gate_rhs[g], preferred_element_type=F32)
        bias = bsp_ref[:, g:g + 1]
        for c in range(n_chunks):
            rs = slice(c * GMLP_CHUNK, (c + 1) * GMLP_CHUNK)
            gate_c = gate[:, c * GMLP_GROUP_DIM:(c + 1) * GMLP_GROUP_DIM] + bias
            ob_ref[0, rs, gs] = (u[rs, gs] * gate_c).astype(BF16)


def _attn_kernel(q_ref, k_ref, vt_ref, lq1_ref, lk1_ref, lq2_ref, lk2_ref, g_ref,
                 wo_ref, wg_ref, wu_ref, wd_ref,
                 o_ref, wo16_ref, wg16_ref, wu16_ref, wd16_ref,
                 m_ref, acc_ref, sa_ref, sb_ref, sc_ref, ma_ref, mb_ref, mc_ref,
                 *, lambda_init):
    for src, dst in ((wo_ref, wo16_ref), (wg_ref, wg16_ref), (wu_ref, wu16_ref),
                     (wd_ref, wd16_ref)):
        dst[...] = src[...].astype(BF16)

    tk = vt_ref.shape[4]
    tq = tk
    n_q = q_ref.shape[2] // tq
    groups = range(0, 2 * tq, ATT_COLS)

    def live_keys(c0, diagonal):
        return min(tk, (c0 & (tq - 1)) + ATT_COLS) if diagonal else tk

    def scores(qi, kt, s_ref, mx_ref, diagonal):
        q = q_ref[0, :, pl.ds(pl.multiple_of(qi * tq, tq), tq), :].reshape(2 * tq, PAIR_DIM)
        k = k_ref[0, 0, pl.ds(pl.multiple_of(kt * tk, tk), tk), :]
        for g, c0 in enumerate(groups):
            cols = slice(c0, c0 + ATT_COLS)
            n_keys = live_keys(c0, diagonal)
            s = lax.dot_general(k[:n_keys], q[cols], (((1,), (1,)), ((), ())),
                                preferred_element_type=F32)
            if diagonal:
                k_chunk = lax.broadcasted_iota(jnp.int32, s.shape, 0) // CHUNK
                q_pos = lax.broadcasted_iota(jnp.int32, s.shape, 1) + (c0 & (tq - 1))
                s = jnp.where(k_chunk <= q_pos // CHUNK, s, MASKED_SCORE)
            s_ref[g, :n_keys] = s
            mx_ref[:, cols] = jnp.max(s, axis=0, keepdims=True)

    def update(kt, s_ref, mx_ref, diagonal):
        vt = vt_ref[0, 0, kt]
        for g, c0 in enumerate(groups):
            cols = slice(c0, c0 + ATT_COLS)
            n_keys = live_keys(c0, diagonal)
            m_prev = m_ref[:, cols]
            m_new = jnp.maximum(m_prev, mx_ref[:, cols])
            alpha = jnp.exp2(m_prev - m_new)
            p = jnp.exp2(s_ref[g, :n_keys] - m_new)
            acc_ref[g] = alpha * acc_ref[g] + jnp.dot(
                vt[:, :n_keys], p.astype(BF16), preferred_element_type=F32)
            m_ref[:, cols] = m_new

    def start_tile():
        m_ref[...] = jnp.full(m_ref.shape, MASKED_SCORE, F32)
        acc_ref[...] = jnp.zeros(acc_ref.shape, F32)

    lam = (jnp.exp(jnp.sum(lq1_ref[...] * lk1_ref[...], axis=1, keepdims=True))
           - jnp.exp(jnp.sum(lq2_ref[...] * lk2_ref[...], axis=1, keepdims=True))
           + lambda_init)

    def finish_tile(qi):
        acc = jnp.concatenate([acc_ref[g] for g in range(len(groups))], axis=1)
        o_all = acc[:PAIR_DIM] / acc[PAIR_DIM:PAIR_DIM + 1]
        o = o_all[:, :tq] - lam * o_all[:, tq:]
        o = o * lax.rsqrt(jnp.mean(o * o, axis=0, keepdims=True) + LN_EPS)
        o = o.T * (g_ref[...] * (1.0 - lambda_init))
        o_ref[0, pl.ds(pl.multiple_of(qi * tq, tq), tq), :] = o.astype(o_ref.dtype)

    start_tile()
    scores(0, 0, sc_ref, mc_ref, True)
    update(0, sc_ref, mc_ref, True)
    second = min(1, n_q - 1)
    scores(second, second, sc_ref, mc_ref, True)

    def query_tile(i, carry):
        nxt = jnp.minimum(i + 1, n_q - 1)
        finish_tile(i - 1)
        start_tile()
        scores(i, 0, sa_ref, ma_ref, False)
        update(i, sc_ref, mc_ref, True)
        bufs = ((sa_ref, ma_ref), (sb_ref, mb_ref))

        def next_scores_then_update(j):
            scores(i, j.value + 1, *bufs[(j.parity + 1) % 2], False)
            update(j.value, *bufs[j.parity], False)

        def unrolled(t, c):
            for u in range(ATT_UNROLL):
                next_scores_then_update(_Tile(ATT_UNROLL * t + u, u % 2))
            return c

        done = (i - 1) // ATT_UNROLL
        lax.fori_loop(0, done, unrolled, 0)

        base = ATT_UNROLL * done
        for left in range(ATT_UNROLL):
            @pl.when((i - 1) % ATT_UNROLL == left)
            def _(left=left):
                for u in range(left):
                    next_scores_then_update(_Tile(base + u, u % 2))
                scores(nxt, nxt, sc_ref, mc_ref, True)
                update(base + left, *bufs[left % 2], False)

        return carry

    lax.fori_loop(1, n_q, query_tile, 0)
    finish_tile(n_q - 1)


def _out_ffn_kernel(x_ref, oa_ref, ob_ref, wo_ref, g1_ref, b1_ref, wg_ref, wu_ref,
                    cw_ref, cb_ref, wd_ref, g2_ref, b2_ref, out_ref, gate_ref, *, alpha):
    rows = x_ref.shape[1]
    d_ff = wg_ref.shape[1]
    seq_tile = pl.program_id(1)
    halves = [slice(0, rows // 2), slice(rows // 2, rows)]
    whole = [slice(0, rows)]

    def mixed(rs):
        cat = jnp.concatenate([oa_ref[0, rs], ob_ref[0, rs]], axis=1)
        mix = jnp.dot(cat, wo_ref[...], preferred_element_type=F32)
        return _layer_norm(alpha * x_ref[0, rs] + mix, g1_ref[...], b1_ref[...])

    y1 = jnp.concatenate([mixed(rs) for rs in halves], axis=0)
    yb = y1.astype(BF16)

    @pl.when(seq_tile == 0)
    def _():
        gate_ref[:SUBLANES, :] = jnp.zeros((SUBLANES, d_ff), F32)

    def ffn_chunk(rs, cs, f_rows):
        g = jnp.dot(yb[rs], wg_ref[:, cs], preferred_element_type=F32)
        up = jnp.dot(yb[rs], wu_ref[:, cs], preferred_element_type=F32)
        gate_ref[SUBLANES + rs.start:SUBLANES + rs.stop, cs] = g
        conv = cb_ref[:, cs] + cw_ref[CONV_WIDTH - 1:CONV_WIDTH, cs] * g
        for tap in range(CONV_WIDTH - 1):
            back = CONV_WIDTH - 1 - tap
            window = slice(SUBLANES + rs.start - back, SUBLANES + rs.stop - back)
            conv = conv + cw_ref[tap:tap + 1, cs] * gate_ref[window, cs]
        half = 0.5 * conv
        h = ((half + half * jnp.tanh(half)) * up).astype(BF16)
        return f_rows + jnp.dot(h, wd_ref[cs, :], preferred_element_type=F32)

    f = jnp.zeros((rows, x_ref.shape[2]), F32)
    bounds = [0, FFN_FIRST_COLS, d_ff - FFN_LAST_COLS, d_ff]
    for c0, c1 in zip(bounds[:-1], bounds[1:]):
        cs = slice(c0, c1)
        parts = halves if c0 == 0 or c1 == d_ff else whole
        f = jnp.concatenate([ffn_chunk(rs, cs, f[rs]) for rs in parts], axis=0)
        gate_ref[:SUBLANES, cs] = gate_ref[rows:rows + SUBLANES, cs]
    for rs in halves:
        out_ref[0, rs] = _layer_norm(alpha * y1[rs] + f[rs], g2_ref[...], b2_ref[...])


def _rope_tables(seq_len):
    pos = jnp.arange(seq_len, dtype=F32)
    inv_freq = 1.0 / (ROPE_THETA ** (jnp.arange(0, DIFF_HEAD_DIM, 2, dtype=F32) / DIFF_HEAD_DIM))
    ang = pos[:, None] * inv_freq[None, :]
    cos, sin = jnp.cos(ang), jnp.sin(ang)
    reps = LANES // (DIFF_HEAD_DIM // 2)
    cos_t = jnp.tile(cos, (1, reps))
    sin_t = jnp.tile(jnp.concatenate([-sin, sin], axis=1), (1, reps // 2))
    return cos_t, sin_t


def _const_spec(shape):
    return pl.BlockSpec(shape, lambda *_: (0,) * len(shape), pipeline_mode=pl.Buffered(1))


def _layer(x, cos_t, sin_t, lambda_init, alpha, w_in, lq1, lk1, lq2, lk2, subln_g, lng, lnb,
           w_sp, b_sp, w_out, ln1_g, ln1_b, w_gate, w_up, conv_w, conv_b, w_down, ln2_g, ln2_b):
    B, S, D = x.shape
    att_width = N_DIFF_HEADS * PAIR_DIM
    gmlp_width = N_GMLP_GROUPS * GMLP_GROUP_DIM
    d_ff = w_gate.shape[1]
    n_pairs = N_DIFF_HEADS
    n_groups = 2 * IN_ROWS // ATT_COLS
    row = lambda a: a.reshape(1, -1)

    q, k, v, ob = pl.pallas_call(
        functools.partial(_in_proj_kernel, att_width=att_width, gmlp_width=gmlp_width),
        grid=(B, S // IN_ROWS),
        in_specs=[
            pl.BlockSpec((1, IN_ROWS, D), lambda b, i: (b, i, 0)),
            _const_spec(w_in.shape),
            pl.BlockSpec((IN_ROWS, LANES), lambda b, i: (i, 0)),
            pl.BlockSpec((IN_ROWS, LANES), lambda b, i: (i, 0)),
            _const_spec((1, gmlp_width)),
            _const_spec((1, gmlp_width)),
            _const_spec(w_sp.shape),
            _const_spec((GMLP_CHUNK, N_GMLP_GROUPS)),
        ],
        out_specs=[
            pl.BlockSpec((1, 2 * n_pairs, IN_ROWS, PAIR_DIM), lambda b, i: (b, 0, i, 0)),
            pl.BlockSpec((1, n_pairs, IN_ROWS, PAIR_DIM), lambda b, i: (b, 0, i, 0)),
            pl.BlockSpec((1, n_pairs, 1, VT_ROWS, IN_ROWS), lambda b, i: (b, 0, i, 0, 0)),
            pl.BlockSpec((1, IN_ROWS, gmlp_width), lambda b, i: (b, i, 0)),
        ],
        out_shape=[
            jax.ShapeDtypeStruct((B, 2 * n_pairs, S, PAIR_DIM), BF16),
            jax.ShapeDtypeStruct((B, n_pairs, S, PAIR_DIM), BF16),
            jax.ShapeDtypeStruct((B, n_pairs, S // IN_ROWS, VT_ROWS, IN_ROWS), BF16),
            jax.ShapeDtypeStruct((B, S, gmlp_width), BF16),
        ],
        scratch_shapes=[pltpu.VMEM(w_in.shape, BF16)],
        compiler_params=pltpu.CompilerParams(
            dimension_semantics=("arbitrary", "arbitrary"), vmem_limit_bytes=VMEM_LIMIT),
        name="in_proj_gmlp",
    )(x, w_in, cos_t, sin_t, row(lng), row(lnb), w_sp, b_sp.T)

    n_steps = B * n_pairs

    def slab_spec(w):
        slab = next(s for s in range(BF16_SUBLANES, w.shape[0] + 1, BF16_SUBLANES)
                    if w.shape[0] % s == 0 and w.shape[0] // s <= n_steps)
        last = w.shape[0] // slab - 1
        return pl.BlockSpec((slab, w.shape[1]),
                            lambda b, p: (jnp.minimum(b * n_pairs + p, last), 0))

    ffn_weights = (w_out, w_gate, w_up, w_down)
    oa, w_out16, w_gate16, w_up16, w_down16 = pl.pallas_call(
        functools.partial(_attn_kernel, lambda_init=lambda_init),
        grid=(B, n_pairs),
        in_specs=[
            pl.BlockSpec((1, 2, S, PAIR_DIM), lambda b, p: (b, p, 0, 0)),
            pl.BlockSpec((1, 1, S, PAIR_DIM), lambda b, p: (b, p, 0, 0)),
            pl.BlockSpec((1, 1, S // IN_ROWS, VT_ROWS, IN_ROWS), lambda b, p: (b, p, 0, 0, 0)),
            _const_spec((1, DIFF_HEAD_DIM)),
            _const_spec((1, DIFF_HEAD_DIM)),
            _const_spec((1, DIFF_HEAD_DIM)),
            _const_spec((1, DIFF_HEAD_DIM)),
            _const_spec((1, PAIR_DIM)),
        ] + [slab_spec(w) for w in ffn_weights],
        out_specs=[pl.BlockSpec((1, S, PAIR_DIM), lambda b, p: (b, 0, p))]
        + [slab_spec(w) for w in ffn_weights],
        out_shape=[jax.ShapeDtypeStruct((B, S, att_width), BF16)]
        + [jax.ShapeDtypeStruct(w.shape, BF16) for w in ffn_weights],
        scratch_shapes=[
            pltpu.VMEM((1, 2 * IN_ROWS), F32),
            pltpu.VMEM((n_groups, VT_ROWS, ATT_COLS), F32),
            pltpu.VMEM((n_groups, IN_ROWS, ATT_COLS), F32),
            pltpu.VMEM((n_groups, IN_ROWS, ATT_COLS), F32),
            pltpu.VMEM((n_groups, IN_ROWS, ATT_COLS), F32),
            pltpu.VMEM((1, 2 * IN_ROWS), F32),
            pltpu.VMEM((1, 2 * IN_ROWS), F32),
            pltpu.VMEM((1, 2 * IN_ROWS), F32),
        ],
        compiler_params=pltpu.CompilerParams(
            dimension_semantics=("arbitrary", "arbitrary"),
            vmem_limit_bytes=VMEM_LIMIT),
        name="diff_attention",
    )(q, k, v, row(lq1), row(lk1), row(lq2), row(lk2), row(subln_g), *ffn_weights)

    out = pl.pallas_call(
        functools.partial(_out_ffn_kernel, alpha=alpha),
        grid=(B, S // FFN_ROWS),
        in_specs=[
            pl.BlockSpec((1, FFN_ROWS, D), lambda b, i: (b, i, 0)),
            pl.BlockSpec((1, FFN_ROWS, att_width), lambda b, i: (b, i, 0)),
            pl.BlockSpec((1, FFN_ROWS, gmlp_width), lambda b, i: (b, i, 0)),
            _const_spec(w_out.shape),
            _const_spec((1, D)),
            _const_spec((1, D)),
            _const_spec(w_gate.shape),
            _const_spec(w_up.shape),
            _const_spec(conv_w.shape),
            _const_spec((1, d_ff)),
            _const_spec(w_down.shape),
            _const_spec((1, D)),
            _const_spec((1, D)),
        ],
        out_specs=pl.BlockSpec((1, FFN_ROWS, D), lambda b, i: (b, i, 0)),
        out_shape=jax.ShapeDtypeStruct((B, S, D), x.dtype),
        scratch_shapes=[pltpu.VMEM((SUBLANES + FFN_ROWS, d_ff), F32)],
        compiler_params=pltpu.CompilerParams(
            dimension_semantics=("arbitrary", "arbitrary"), vmem_limit_bytes=VMEM_LIMIT),
        name="out_proj_ffn",
    )(x, oa, ob, w_out16, row(ln1_g), row(ln1_b), w_gate16, w_up16, conv_w, row(conv_b),
      w_down16, row(ln2_g), row(ln2_b))
    return out


def kernel(x, w_in, lambda_q1, lambda_k1, lambda_q2, lambda_k2, subln_g, gmlp_ln_g, gmlp_ln_b,
           w_spatial, b_spatial, w_out, ln1_g, ln1_b, w_gate, w_up, conv_w, conv_b, w_down,
           ln2_g, ln2_b):
    depth = w_in.shape[0]
    alpha = (2 * depth) ** 0.25
    cos_t, sin_t = _rope_tables(x.shape[1])
    for l in range(depth):
        lambda_init = 0.8 - 0.6 * math.exp(-0.3 * l)
        x = _layer(x, cos_t, sin_t, lambda_init, alpha, w_in[l], lambda_q1[l], lambda_k1[l],
                   lambda_q2[l], lambda_k2[l], subln_g[l], gmlp_ln_g[l], gmlp_ln_b[l],
                   w_spatial[l], b_spatial[l], w_out[l], ln1_g[l], ln1_b[l], w_gate[l],
                   w_up[l], conv_w[l], conv_b[l], w_down[l], ln2_g[l], ln2_b[l])
    return x
```

```python
import functools
import math
from typing import Any, NamedTuple

import jax
import jax.numpy as jnp
from jax import lax
from jax.experimental import pallas as pl
from jax.experimental.pallas import tpu as pltpu

F32 = jnp.float32
BF16 = jnp.bfloat16

CHUNK = 64
N_DIFF_HEADS = 4
DIFF_HEAD_DIM = 64
PAIR_DIM = 2 * DIFF_HEAD_DIM
N_GMLP_GROUPS = 4
GMLP_GROUP_DIM = 128
GMLP_CHUNK = 128
CONV_WIDTH = 3
ROPE_THETA = 10000.0
LN_EPS = 1e-5

LANES = 128
SUBLANES = 8
BF16_SUBLANES = 16
VT_ROWS = PAIR_DIM + BF16_SUBLANES
LOG2_E = math.log2(math.e)

IN_ROWS = 512
ATT_COLS = 256
ATT_UNROLL = 8
FFN_ROWS = 512
FFN_FIRST_COLS = 1024
FFN_LAST_COLS = 768
MASKED_SCORE = -1e30
VMEM_LIMIT = 56 * 1024 * 1024


class _Tile(NamedTuple):
    value: Any
    parity: int


def _layer_norm(z, g, b):
    mu = jnp.mean(z, axis=-1, keepdims=True)
    d = z - mu
    var = jnp.mean(d * d, axis=-1, keepdims=True)
    return d * lax.rsqrt(var + LN_EPS) * g + b


def _gelu(z):
    return 0.5 * z * (1.0 + lax.erf(z * (1.0 / math.sqrt(2.0))))


def _in_proj_kernel(x_ref, w_ref, cos_ref, sin_ref, lng_ref, lnb_ref, wsp_ref, bsp_ref,
                    q_ref, k_ref, vt_ref, ob_ref, w16_ref, *, att_width, gmlp_width):
    @pl.when((pl.program_id(0) == 0) & (pl.program_id(1) == 0))
    def _():
        w16_ref[...] = w_ref[...].astype(BF16)

    rows = x_ref.shape[1]
    xb = x_ref[0].astype(BF16)

    def proj(col0, width):
        return jnp.dot(xb, w16_ref[:, col0:col0 + width], preferred_element_type=F32)

    cos = cos_ref[...]
    sin = sin_ref[...]
    lane = lax.broadcasted_iota(jnp.int32, (rows, LANES), 1)
    first_half = (lane & (DIFF_HEAD_DIM // 2)) == 0
    low_head = lane < DIFF_HEAD_DIM

    def rope(t):
        rot = jnp.where(first_half, pltpu.roll(t, LANES - DIFF_HEAD_DIM // 2, 1),
                        pltpu.roll(t, DIFF_HEAD_DIM // 2, 1))
        return t * cos + rot * sin

    vg = _gelu(proj(3 * att_width + gmlp_width, gmlp_width))
    u = _gelu(proj(3 * att_width, gmlp_width))
    n_chunks = rows // GMLP_CHUNK
    gate_rhs = []
    for g in range(N_GMLP_GROUPS):
        gs = slice(g * GMLP_GROUP_DIM, (g + 1) * GMLP_GROUP_DIM)
        vn = _layer_norm(vg[:, gs], lng_ref[:, gs], lnb_ref[:, gs]).astype(BF16)
        gate_rhs.append(jnp.concatenate(
            [vn[c * GMLP_CHUNK:(c + 1) * GMLP_CHUNK] for c in range(n_chunks)], axis=1))

    n_pairs = att_width // PAIR_DIM
    hq = proj(0, att_width)
    for p in range(n_pairs):
        r = rope(hq[:, p * PAIR_DIM:(p + 1) * PAIR_DIM]) * (DIFF_HEAD_DIM ** -0.5 * LOG2_E)
        q_ref[0, 2 * p] = jnp.where(low_head, r, 0.0).astype(BF16)
        q_ref[0, 2 * p + 1] = jnp.where(low_head, 0.0, r).astype(BF16)
    hk = proj(att_width, att_width)
    for p in range(n_pairs):
        k_ref[0, p] = rope(hk[:, p * PAIR_DIM:(p + 1) * PAIR_DIM]).astype(BF16)
    hv = proj(2 * att_width, att_width)
    for p in range(n_pairs):
        vt_ref[0, p, 0, :PAIR_DIM, :] = hv[:, p * PAIR_DIM:(p + 1) * PAIR_DIM].T.astype(BF16)
        extra = lax.broadcasted_iota(jnp.int32, (BF16_SUBLANES, rows), 0)
        vt_ref[0, p, 0, PAIR_DIM:, :] = jnp.where(extra == 0, 1.0, 0.0).astype(BF16)

    t_idx = lax.broadcasted_iota(jnp.int32, (GMLP_CHUNK, GMLP_CHUNK), 0)
    s_idx = lax.broadcasted_iota(jnp.int32, (GMLP_CHUNK, GMLP_CHUNK), 1)
    for g in range(N_GMLP_GROUPS):
        gs = slice(g * GMLP_GROUP_DIM, (g + 1) * GMLP_GROUP_DIM)
        w_causal = jnp.where(t_idx >= s_idx, wsp_ref[g], 0.0).astype(BF16)
        gate = jnp.dot(w_causal, gate_rhs[g], preferred_element_type=F32)
        bias = bsp_ref[:, g:g + 1]
        for c in range(n_chunks):
            rs = slice(c * GMLP_CHUNK, (c + 1) * GMLP_CHUNK)
            gate_c = gate[:, c * GMLP_GROUP_DIM:(c + 1) * GMLP_GROUP_DIM] + bias
            ob_ref[0, rs, gs] = (u[rs, gs] * gate_c).astype(BF16)


def _attn_kernel(q_ref, k_ref, vt_ref, lq1_ref, lk1_ref, lq2_ref, lk2_ref, g_ref,
                 wo_ref, wg_ref, wu_ref, wd_ref,
                 o_ref, wo16_ref, wg16_ref, wu16_ref, wd16_ref,
                 m_ref, acc_ref, sa_ref, sb_ref, sc_ref, ma_ref, mb_ref, mc_ref,
                 *, lambda_init):
    for src, dst in ((wo_ref, wo16_ref), (wg_ref, wg16_ref), (wu_ref, wu16_ref),
                     (wd_ref, wd16_ref)):
        dst[...] = src[...].astype(BF16)

    tk = vt_ref.shape[4]
    tq = tk
    n_q = q_ref.shape[2] // tq
    groups = range(0, 2 * tq, ATT_COLS)

    def live_keys(c0, diagonal):
        return min(tk, (c0 & (tq - 1)) + ATT_COLS) if diagonal else tk

    def scores(qi, kt, s_ref, mx_ref, diagonal):
        q = q_ref[0, :, pl.ds(pl.multiple_of(qi * tq, tq), tq), :].reshape(2 * tq, PAIR_DIM)
        k = k_ref[0, 0, pl.ds(pl.multiple_of(kt * tk, tk), tk), :]
        for g, c0 in enumerate(groups):
            cols = slice(c0, c0 + ATT_COLS)
            n_keys = live_keys(c0, diagonal)
            s = lax.dot_general(k[:n_keys], q[cols], (((1,), (1,)), ((), ())),
                                preferred_element_type=F32)
            if diagonal:
                k_chunk = lax.broadcasted_iota(jnp.int32, s.shape, 0) // CHUNK
                q_pos = lax.broadcasted_iota(jnp.int32, s.shape, 1) + (c0 & (tq - 1))
                s = jnp.where(k_chunk <= q_pos // CHUNK, s, MASKED_SCORE)
            s_ref[g, :n_keys] = s
            mx_ref[:, cols] = jnp.max(s, axis=0, keepdims=True)

    def update(kt, s_ref, mx_ref, diagonal):
        vt = vt_ref[0, 0, kt]
        for g, c0 in enumerate(groups):
            cols = slice(c0, c0 + ATT_COLS)
            n_keys = live_keys(c0, diagonal)
            m_prev = m_ref[:, cols]
            m_new = jnp.maximum(m_prev, mx_ref[:, cols])
            alpha = jnp.exp2(m_prev - m_new)
            p = jnp.exp2(s_ref[g, :n_keys] - m_new)
            acc_ref[g] = alpha * acc_ref[g] + jnp.dot(
                vt[:, :n_keys], p.astype(BF16), preferred_element_type=F32)
            m_ref[:, cols] = m_new

    def start_tile():
        m_ref[...] = jnp.full(m_ref.shape, MASKED_SCORE, F32)
        acc_ref[...] = jnp.zeros(acc_ref.shape, F32)

    lam = (jnp.exp(jnp.sum(lq1_ref[...] * lk1_ref[...], axis=1, keepdims=True))
           - jnp.exp(jnp.sum(lq2_ref[...] * lk2_ref[...], axis=1, keepdims=True))
           + lambda_init)

    def finish_tile(qi):
        acc = jnp.concatenate([acc_ref[g] for g in range(len(groups))], axis=1)
        o_all = acc[:PAIR_DIM] / acc[PAIR_DIM:PAIR_DIM + 1]
        o = o_all[:, :tq] - lam * o_all[:, tq:]
        o = o * lax.rsqrt(jnp.mean(o * o, axis=0, keepdims=True) + LN_EPS)
        o = o.T * (g_ref[...] * (1.0 - lambda_init))
        o_ref[0, pl.ds(pl.multiple_of(qi * tq, tq), tq), :] = o.astype(o_ref.dtype)

    start_tile()
    scores(0, 0, sc_ref, mc_ref, True)
    update(0, sc_ref, mc_ref, True)
    second = min(1, n_q - 1)
    scores(second, second, sc_ref, mc_ref, True)

    def query_tile(i, carry):
        nxt = jnp.minimum(i + 1, n_q - 1)
        finish_tile(i - 1)
        start_tile()
        scores(i, 0, sa_ref, ma_ref, False)
        update(i, sc_ref, mc_ref, True)
        bufs = ((sa_ref, ma_ref), (sb_ref, mb_ref))

        def next_scores_then_update(j):
            scores(i, j.value + 1, *bufs[(j.parity + 1) % 2], False)
            update(j.value, *bufs[j.parity], False)

        def unrolled(t, c):
            for u in range(ATT_UNROLL):
                next_scores_then_update(_Tile(ATT_UNROLL * t + u, u % 2))
            return c

        done = (i - 1) // ATT_UNROLL
        lax.fori_loop(0, done, unrolled, 0)

        base = ATT_UNROLL * done
        for left in range(ATT_UNROLL):
            @pl.when((i - 1) % ATT_UNROLL == left)
            def _(left=left):
                for u in range(left):
                    next_scores_then_update(_Tile(base + u, u % 2))
                scores(nxt, nxt, sc_ref, mc_ref, True)
                update(base + left, *bufs[left % 2], False)

        return carry

    lax.fori_loop(1, n_q, query_tile, 0)
    finish_tile(n_q - 1)


def _out_ffn_kernel(x_ref, oa_ref, ob_ref, wo_ref, g1_ref, b1_ref, wg_ref, wu_ref,
                    cw_ref, cb_ref, wd_ref, g2_ref, b2_ref, out_ref, gate_ref, *, alpha):
    rows = x_ref.shape[1]
    d_ff = wg_ref.shape[1]
    seq_tile = pl.program_id(1)
    halves = [slice(0, rows // 2), slice(rows // 2, rows)]
    whole = [slice(0, rows)]

    def mixed(rs):
        cat = jnp.concatenate([oa_ref[0, rs], ob_ref[0, rs]], axis=1)
        mix = jnp.dot(cat, wo_ref[...], preferred_element_type=F32)
        return _layer_norm(alpha * x_ref[0, rs] + mix, g1_ref[...], b1_ref[...])

    y1 = jnp.concatenate([mixed(rs) for rs in halves], axis=0)
    yb = y1.astype(BF16)

    @pl.when(seq_tile == 0)
    def _():
        gate_ref[:SUBLANES, :] = jnp.zeros((SUBLANES, d_ff), F32)

    def ffn_chunk(rs, cs, f_rows):
        g = jnp.dot(yb[rs], wg_ref[:, cs], preferred_element_type=F32)
        up = jnp.dot(yb[rs], wu_ref[:, cs], preferred_element_type=F32)
        gate_ref[SUBLANES + rs.start:SUBLANES + rs.stop, cs] = g
        conv = cb_ref[:, cs] + cw_ref[CONV_WIDTH - 1:CONV_WIDTH, cs] * g
        for tap in range(CONV_WIDTH - 1):
            back = CONV_WIDTH - 1 - tap
            window = slice(SUBLANES + rs.start - back, SUBLANES + rs.stop - back)
            conv = conv + cw_ref[tap:tap + 1, cs] * gate_ref[window, cs]
        half = 0.5 * conv
        h = ((half + half * jnp.tanh(half)) * up).astype(BF16)
        return f_rows + jnp.dot(h, wd_ref[cs, :], preferred_element_type=F32)

    f = jnp.zeros((rows, x_ref.shape[2]), F32)
    bounds = [0, FFN_FIRST_COLS, d_ff - FFN_LAST_COLS, d_ff]
    for c0, c1 in zip(bounds[:-1], bounds[1:]):
        cs = slice(c0, c1)
        parts = halves if c0 == 0 or c1 == d_ff else whole
        f = jnp.concatenate([ffn_chunk(rs, cs, f[rs]) for rs in parts], axis=0)
        gate_ref[:SUBLANES, cs] = gate_ref[rows:rows + SUBLANES, cs]
    for rs in halves:
        out_ref[0, rs] = _layer_norm(alpha * y1[rs] + f[rs], g2_ref[...], b2_ref[...])


def _rope_tables(seq_len):
    pos = jnp.arange(seq_len, dtype=F32)
    inv_freq = 1.0 / (ROPE_THETA ** (jnp.arange(0, DIFF_HEAD_DIM, 2, dtype=F32) / DIFF_HEAD_DIM))
    ang = pos[:, None] * inv_freq[None, :]
    cos, sin = jnp.cos(ang), jnp.sin(ang)
    reps = LANES // (DIFF_HEAD_DIM // 2)
    cos_t = jnp.tile(cos, (1, reps))
    sin_t = jnp.tile(jnp.concatenate([-sin, sin], axis=1), (1, reps // 2))
    return cos_t, sin_t


def _const_spec(shape):
    return pl.BlockSpec(shape, lambda *_: (0,) * len(shape), pipeline_mode=pl.Buffered(1))


def _layer(x, cos_t, sin_t, lambda_init, alpha, w_in, lq1, lk1, lq2, lk2, subln_g, lng, lnb,
           w_sp, b_sp, w_out, ln1_g, ln1_b, w_gate, w_up, conv_w, conv_b, w_down, ln2_g, ln2_b):
    B, S, D = x.shape
    att_width = N_DIFF_HEADS * PAIR_DIM
    gmlp_width = N_GMLP_GROUPS * GMLP_GROUP_DIM
    d_ff = w_gate.shape[1]
    n_pairs = N_DIFF_HEADS
    n_groups = 2 * IN_ROWS // ATT_COLS
    row = lambda a: a.reshape(1, -1)

    q, k, v, ob = pl.pallas_call(
        functools.partial(_in_proj_kernel, att_width=att_width, gmlp_width=gmlp_width),
        grid=(B, S // IN_ROWS),
        in_specs=[
            pl.BlockSpec((1, IN_ROWS, D), lambda b, i: (b, i, 0)),
            _const_spec(w_in.shape),
            pl.BlockSpec((IN_ROWS, LANES), lambda b, i: (i, 0)),
            pl.BlockSpec((IN_ROWS, LANES), lambda b, i: (i, 0)),
            _const_spec((1, gmlp_width)),
            _const_spec((1, gmlp_width)),
            _const_spec(w_sp.shape),
            _const_spec((GMLP_CHUNK, N_GMLP_GROUPS)),
        ],
        out_specs=[
            pl.BlockSpec((1, 2 * n_pairs, IN_ROWS, PAIR_DIM), lambda b, i: (b, 0, i, 0)),
            pl.BlockSpec((1, n_pairs, IN_ROWS, PAIR_DIM), lambda b, i: (b, 0, i, 0)),
            pl.BlockSpec((1, n_pairs, 1, VT_ROWS, IN_ROWS), lambda b, i: (b, 0, i, 0, 0)),
            pl.BlockSpec((1, IN_ROWS, gmlp_width), lambda b, i: (b, i, 0)),
        ],
        out_shape=[
            jax.ShapeDtypeStruct((B, 2 * n_pairs, S, PAIR_DIM), BF16),
            jax.ShapeDtypeStruct((B, n_pairs, S, PAIR_DIM), BF16),
            jax.ShapeDtypeStruct((B, n_pairs, S // IN_ROWS, VT_ROWS, IN_ROWS), BF16),
            jax.ShapeDtypeStruct((B, S, gmlp_width), BF16),
        ],
        scratch_shapes=[pltpu.VMEM(w_in.shape, BF16)],
        compiler_params=pltpu.CompilerParams(
            dimension_semantics=("arbitrary", "arbitrary"), vmem_limit_bytes=VMEM_LIMIT),
        name="in_proj_gmlp",
    )(x, w_in, cos_t, sin_t, row(lng), row(lnb), w_sp, b_sp.T)

    n_steps = B * n_pairs

    def slab_spec(w):
        slab = next(s for s in range(BF16_SUBLANES, w.shape[0] + 1, BF16_SUBLANES)
                    if w.shape[0] % s == 0 and w.shape[0] // s <= n_steps)
        last = w.shape[0] // slab - 1
        return pl.BlockSpec((slab, w.shape[1]),
                            lambda b, p: (jnp.minimum(b * n_pairs + p, last), 0))

    ffn_weights = (w_out, w_gate, w_up, w_down)
    oa, w_out16, w_gate16, w_up16, w_down16 = pl.pallas_call(
        functools.partial(_attn_kernel, lambda_init=lambda_init),
        grid=(B, n_pairs),
        in_specs=[
            pl.BlockSpec((1, 2, S, PAIR_DIM), lambda b, p: (b, p, 0, 0)),
            pl.BlockSpec((1, 1, S, PAIR_DIM), lambda b, p: (b, p, 0, 0)),
            pl.BlockSpec((1, 1, S // IN_ROWS, VT_ROWS, IN_ROWS), lambda b, p: (b, p, 0, 0, 0)),
            _const_spec((1, DIFF_HEAD_DIM)),
            _const_spec((1, DIFF_HEAD_DIM)),
            _const_spec((1, DIFF_HEAD_DIM)),
            _const_spec((1, DIFF_HEAD_DIM)),
            _const_spec((1, PAIR_DIM)),
        ] + [slab_spec(w) for w in ffn_weights],
        out_specs=[pl.BlockSpec((1, S, PAIR_DIM), lambda b, p: (b, 0, p))]
        + [slab_spec(w) for w in ffn_weights],
        out_shape=[jax.ShapeDtypeStruct((B, S, att_width), BF16)]
        + [jax.ShapeDtypeStruct(w.shape, BF16) for w in ffn_weights],
        scratch_shapes=[
            pltpu.VMEM((1, 2 * IN_ROWS), F32),
            pltpu.VMEM((n_groups, VT_ROWS, ATT_COLS), F32),
            pltpu.VMEM((n_groups, IN_ROWS, ATT_COLS), F32),
            pltpu.VMEM((n_groups, IN_ROWS, ATT_COLS), F32),
            pltpu.VMEM((n_groups, IN_ROWS, ATT_COLS), F32),
            pltpu.VMEM((1, 2 * IN_ROWS), F32),
            pltpu.VMEM((1, 2 * IN_ROWS), F32),
            pltpu.VMEM((1, 2 * IN_ROWS), F32),
        ],
        compiler_params=pltpu.CompilerParams(
            dimension_semantics=("arbitrary", "arbitrary"),
            vmem_limit_bytes=VMEM_LIMIT),
        name="diff_attention",
    )(q, k, v, row(lq1), row(lk1), row(lq2), row(lk2), row(subln_g), *ffn_weights)

    out = pl.pallas_call(
        functools.partial(_out_ffn_kernel, alpha=alpha),
        grid=(B, S // FFN_ROWS),
        in_specs=[
            pl.BlockSpec((1, FFN_ROWS, D), lambda b, i: (b, i, 0)),
            pl.BlockSpec((1, FFN_ROWS, att_width), lambda b, i: (b, i, 0)),
            pl.BlockSpec((1, FFN_ROWS, gmlp_width), lambda b, i: (b, i, 0)),
            _const_spec(w_out.shape),
            _const_spec((1, D)),
            _const_spec((1, D)),
            _const_spec(w_gate.shape),
            _const_spec(w_up.shape),
            _const_spec(conv_w.shape),
            _const_spec((1, d_ff)),
            _const_spec(w_down.shape),
            _const_spec((1, D)),
            _const_spec((1, D)),
        ],
        out_specs=pl.BlockSpec((1, FFN_ROWS, D), lambda b, i: (b, i, 0)),
        out_shape=jax.ShapeDtypeStruct((B, S, D), x.dtype),
        scratch_shapes=[pltpu.VMEM((SUBLANES + FFN_ROWS, d_ff), F32)],
        compiler_params=pltpu.CompilerParams(
            dimension_semantics=("arbitrary", "arbitrary"), vmem_limit_bytes=VMEM_LIMIT),
        name="out_proj_ffn",
    )(x, oa, ob, w_out16, row(ln1_g), row(ln1_b), w_gate16, w_up16, conv_w, row(conv_b),
      w_down16, row(ln2_g), row(ln2_b))
    return out


def kernel(x, w_in, lambda_q1, lambda_k1, lambda_q2, lambda_k2, subln_g, gmlp_ln_g, gmlp_ln_b,
           w_spatial, b_spatial, w_out, ln1_g, ln1_b, w_gate, w_up, conv_w, conv_b, w_down,
           ln2_g, ln2_b):
    depth = w_in.shape[0]
    alpha = (2 * depth) ** 0.25
    cos_t, sin_t = _rope_tables(x.shape[1])
    for l in range(depth):
        lambda_init = 0.8 - 0.6 * math.exp(-0.3 * l)
        x = _layer(x, cos_t, sin_t, lambda_init, alpha, w_in[l], lambda_q1[l], lambda_k1[l],
                   lambda_q2[l], lambda_k2[l], subln_g[l], gmlp_ln_g[l], gmlp_ln_b[l],
                   w_spatial[l], b_spatial[l], w_out[l], ln1_g[l], ln1_b[l], w_gate[l],
                   w_up[l], conv_w[l], conv_b[l], w_down[l], ln2_g[l], ln2_b[l])
    return x
```

```python
import functools
import math
from typing import Any, NamedTuple

import jax
import jax.numpy as jnp
from jax import lax
from jax.experimental import pallas as pl
from jax.experimental.pallas import tpu as pltpu

F32 = jnp.float32
BF16 = jnp.bfloat16

CHUNK = 64
N_DIFF_HEADS = 4
DIFF_HEAD_DIM = 64
PAIR_DIM = 2 * DIFF_HEAD_DIM
N_GMLP_GROUPS = 4
GMLP_GROUP_DIM = 128
GMLP_CHUNK = 128
CONV_WIDTH = 3
ROPE_THETA = 10000.0
LN_EPS = 1e-5

LANES = 128
SUBLANES = 8
BF16_SUBLANES = 16
VT_ROWS = PAIR_DIM + BF16_SUBLANES
LOG2_E = math.log2(math.e)

IN_ROWS = 512
ATT_COLS = 256
ATT_UNROLL = 6
FFN_ROWS = 512
FFN_FIRST_COLS = 1024
FFN_LAST_COLS = 768
MASKED_SCORE = -1e30
VMEM_LIMIT = 56 * 1024 * 1024


class _Tile(NamedTuple):
    value: Any
    parity: int


def _layer_norm(z, g, b):
    mu = jnp.mean(z, axis=-1, keepdims=True)
    d = z - mu
    var = jnp.mean(d * d, axis=-1, keepdims=True)
    return d * lax.rsqrt(var + LN_EPS) * g + b


def _gelu(z):
    return 0.5 * z * (1.0 + lax.erf(z * (1.0 / math.sqrt(2.0))))


def _in_proj_kernel(x_ref, w_ref, cos_ref, sin_ref, lng_ref, lnb_ref, wsp_ref, bsp_ref,
                    q_ref, k_ref, vt_ref, ob_ref, w16_ref, *, att_width, gmlp_width):
    @pl.when((pl.program_id(0) == 0) & (pl.program_id(1) == 0))
    def _():
        w16_ref[...] = w_ref[...].astype(BF16)

    rows = x_ref.shape[1]
    xb = x_ref[0].astype(BF16)

    def proj(col0, width):
        return jnp.dot(xb, w16_ref[:, col0:col0 + width], preferred_element_type=F32)

    cos = cos_ref[...]
    sin = sin_ref[...]
    lane = lax.broadcasted_iota(jnp.int32, (rows, LANES), 1)
    first_half = (lane & (DIFF_HEAD_DIM // 2)) == 0
    low_head = lane < DIFF_HEAD_DIM

    def rope(t):
        rot = jnp.where(first_half, pltpu.roll(t, LANES - DIFF_HEAD_DIM // 2, 1),
                        pltpu.roll(t, DIFF_HEAD_DIM // 2, 1))
        return t * cos + rot * sin

    vg = _gelu(proj(3 * att_width + gmlp_width, gmlp_width))
    u = _gelu(proj(3 * att_width, gmlp_width))
    n_chunks = rows // GMLP_CHUNK
    gate_rhs = []
    for g in range(N_GMLP_GROUPS):
        gs = slice(g * GMLP_GROUP_DIM, (g + 1) * GMLP_GROUP_DIM)
        vn = _layer_norm(vg[:, gs], lng_ref[:, gs], lnb_ref[:, gs]).astype(BF16)
        gate_rhs.append(jnp.concatenate(
            [vn[c * GMLP_CHUNK:(c + 1) * GMLP_CHUNK] for c in range(n_chunks)], axis=1))

    n_pairs = att_width // PAIR_DIM
    hq = proj(0, att_width)
    for p in range(n_pairs):
        r = rope(hq[:, p * PAIR_DIM:(p + 1) * PAIR_DIM]) * (DIFF_HEAD_DIM ** -0.5 * LOG2_E)
        q_ref[0, 2 * p] = jnp.where(low_head, r, 0.0).astype(BF16)
        q_ref[0, 2 * p + 1] = jnp.where(low_head, 0.0, r).astype(BF16)
    hk = proj(att_width, att_width)
    for p in range(n_pairs):
        k_ref[0, p] = rope(hk[:, p * PAIR_DIM:(p + 1) * PAIR_DIM]).astype(BF16)
    hv = proj(2 * att_width, att_width)
    for p in range(n_pairs):
        vt_ref[0, p, 0, :PAIR_DIM, :] = hv[:, p * PAIR_DIM:(p + 1) * PAIR_DIM].T.astype(BF16)
        extra = lax.broadcasted_iota(jnp.int32, (BF16_SUBLANES, rows), 0)
        vt_ref[0, p, 0, PAIR_DIM:, :] = jnp.where(extra == 0, 1.0, 0.0).astype(BF16)

    t_idx = lax.broadcasted_iota(jnp.int32, (GMLP_CHUNK, GMLP_CHUNK), 0)
    s_idx = lax.broadcasted_iota(jnp.int32, (GMLP_CHUNK, GMLP_CHUNK), 1)
    for g in range(N_GMLP_GROUPS):
        gs = slice(g * GMLP_GROUP_DIM, (g + 1) * GMLP_GROUP_DIM)
        w_causal = jnp.where(t_idx >= s_idx, wsp_ref[g], 0.0).astype(BF16)
        gate = jnp.dot(w_causal, gate_rhs[g], preferred_element_type=F32)
        bias = bsp_ref[:, g:g + 1]
        for c in range(n_chunks):
            rs = slice(c * GMLP_CHUNK, (c + 1) * GMLP_CHUNK)
            gate_c = gate[:, c * GMLP_GROUP_DIM:(c + 1) * GMLP_GROUP_DIM] + bias
            ob_ref[0, rs, gs] = (u[rs, gs] * gate_c).astype(BF16)


def _attn_kernel(q_ref, k_ref, vt_ref, lq1_ref, lk1_ref, lq2_ref, lk2_ref, g_ref,
                 wo_ref, wg_ref, wu_ref, wd_ref,
                 o_ref, wo16_ref, wg16_ref, wu16_ref, wd16_ref,
                 m_ref, acc_ref, sa_ref, sb_ref, sc_ref, ma_ref, mb_ref, mc_ref,
                 *, lambda_init):
    for src, dst in ((wo_ref, wo16_ref), (wg_ref, wg16_ref), (wu_ref, wu16_ref),
                     (wd_ref, wd16_ref)):
        dst[...] = src[...].astype(BF16)

    tk = vt_ref.shape[4]
    tq = tk
    n_q = q_ref.shape[2] // tq
    groups = range(0, 2 * tq, ATT_COLS)

    def live_keys(c0, diagonal):
        return min(tk, (c0 & (tq - 1)) + ATT_COLS) if diagonal else tk

    def scores(qi, kt, s_ref, mx_ref, diagonal, only=None):
        q = q_ref[0, :, pl.ds(pl.multiple_of(qi * tq, tq), tq), :].reshape(2 * tq, PAIR_DIM)
        k = k_ref[0, 0, pl.ds(pl.multiple_of(kt * tk, tk), tk), :]
        for g, c0 in enumerate(groups):
            if only is not None and g != only:
                continue
            cols = slice(c0, c0 + ATT_COLS)
            n_keys = live_keys(c0, diagonal)
            s = lax.dot_general(k[:n_keys], q[cols], (((1,), (1,)), ((), ())),
                                preferred_element_type=F32)
            if diagonal:
                k_chunk = lax.broadcasted_iota(jnp.int32, s.shape, 0) // CHUNK
                q_pos = lax.broadcasted_iota(jnp.int32, s.shape, 1) + (c0 & (tq - 1))
                s = jnp.where(k_chunk <= q_pos // CHUNK, s, MASKED_SCORE)
            s_ref[g, :n_keys] = s
            mx_ref[:, cols] = jnp.max(s, axis=0, keepdims=True)

    def update(kt, s_ref, mx_ref, diagonal, only=None):
        vt = vt_ref[0, 0, kt]
        for g, c0 in enumerate(groups):
            if only is not None and g != only:
                continue
            cols = slice(c0, c0 + ATT_COLS)
            n_keys = live_keys(c0, diagonal)
            m_prev = m_ref[:, cols]
            m_new = jnp.maximum(m_prev, mx_ref[:, cols])
            alpha = jnp.exp2(m_prev - m_new)
            p = jnp.exp2(s_ref[g, :n_keys] - m_new)
            acc_ref[g] = alpha * acc_ref[g] + jnp.dot(
                vt[:, :n_keys], p.astype(BF16), preferred_element_type=F32)
            m_ref[:, cols] = m_new

    def start_tile():
        m_ref[...] = jnp.full(m_ref.shape, MASKED_SCORE, F32)
        acc_ref[...] = jnp.zeros(acc_ref.shape, F32)

    lam = (jnp.exp(jnp.sum(lq1_ref[...] * lk1_ref[...], axis=1, keepdims=True))
           - jnp.exp(jnp.sum(lq2_ref[...] * lk2_ref[...], axis=1, keepdims=True))
           + lambda_init)

    def finish_tile(qi):
        acc = jnp.concatenate([acc_ref[g] for g in range(len(groups))], axis=1)
        o_all = acc[:PAIR_DIM] / acc[PAIR_DIM:PAIR_DIM + 1]
        o = o_all[:, :tq] - lam * o_all[:, tq:]
        o = o * lax.rsqrt(jnp.mean(o * o, axis=0, keepdims=True) + LN_EPS)
        o = o.T * (g_ref[...] * (1.0 - lambda_init))
        o_ref[0, pl.ds(pl.multiple_of(qi * tq, tq), tq), :] = o.astype(o_ref.dtype)

    start_tile()
    scores(0, 0, sc_ref, mc_ref, True)
    update(0, sc_ref, mc_ref, True)
    second = min(1, n_q - 1)
    scores(second, second, sc_ref, mc_ref, True)

    def query_tile(i, carry):
        nxt = jnp.minimum(i + 1, n_q - 1)
        finish_tile(i - 1)
        start_tile()
        for g in range(len(groups)):
            scores(i, 0, sa_ref, ma_ref, False, only=g)
            update(i, sc_ref, mc_ref, True, only=g)
        bufs = ((sa_ref, ma_ref), (sb_ref, mb_ref))

        def next_scores_then_update(j):
            for g in range(len(groups)):
                scores(i, j.value + 1, *bufs[(j.parity + 1) % 2], False, only=g)
                update(j.value, *bufs[j.parity], False, only=g)

        def unrolled(t, c):
            for u in range(ATT_UNROLL):
                next_scores_then_update(_Tile(ATT_UNROLL * t + u, u % 2))
            return c

        done = (i - 1) // ATT_UNROLL
        lax.fori_loop(0, done, unrolled, 0)

        base = ATT_UNROLL * done
        for left in range(ATT_UNROLL):
            @pl.when((i - 1) % ATT_UNROLL == left)
            def _(left=left):
                for u in range(left):
                    next_scores_then_update(_Tile(base + u, u % 2))
                for g in range(len(groups)):
                    scores(nxt, nxt, sc_ref, mc_ref, True, only=g)
                    update(base + left, *bufs[left % 2], False, only=g)

        return carry

    lax.fori_loop(1, n_q, query_tile, 0)
    finish_tile(n_q - 1)


def _out_ffn_kernel(x_ref, oa_ref, ob_ref, wo_ref, g1_ref, b1_ref, wg_ref, wu_ref,
                    cw_ref, cb_ref, wd_ref, g2_ref, b2_ref, out_ref, gate_ref, *, alpha):
    rows = x_ref.shape[1]
    d_ff = wg_ref.shape[1]
    seq_tile = pl.program_id(1)
    halves = [slice(0, rows // 2), slice(rows // 2, rows)]
    whole = [slice(0, rows)]

    def mixed(rs):
        cat = jnp.concatenate([oa_ref[0, rs], ob_ref[0, rs]], axis=1)
        mix = jnp.dot(cat, wo_ref[...], preferred_element_type=F32)
        return _layer_norm(alpha * x_ref[0, rs] + mix, g1_ref[...], b1_ref[...])

    y1 = jnp.concatenate([mixed(rs) for rs in halves], axis=0)
    yb = y1.astype(BF16)

    @pl.when(seq_tile == 0)
    def _():
        gate_ref[:SUBLANES, :] = jnp.zeros((SUBLANES, d_ff), F32)

    def ffn_chunk(rs, cs, f_rows):
        g = jnp.dot(yb[rs], wg_ref[:, cs], preferred_element_type=F32)
        up = jnp.dot(yb[rs], wu_ref[:, cs], preferred_element_type=F32)
        gate_ref[SUBLANES + rs.start:SUBLANES + rs.stop, cs] = g
        conv = cb_ref[:, cs] + cw_ref[CONV_WIDTH - 1:CONV_WIDTH, cs] * g
        for tap in range(CONV_WIDTH - 1):
            back = CONV_WIDTH - 1 - tap
            window = slice(SUBLANES + rs.start - back, SUBLANES + rs.stop - back)
            conv = conv + cw_ref[tap:tap + 1, cs] * gate_ref[window, cs]
        half = 0.5 * conv
        h = ((half + half * jnp.tanh(half)) * up).astype(BF16)
        return f_rows + jnp.dot(h, wd_ref[cs, :], preferred_element_type=F32)

    f = jnp.zeros((rows, x_ref.shape[2]), F32)
    bounds = [0, FFN_FIRST_COLS, d_ff - FFN_LAST_COLS, d_ff]
    for c0, c1 in zip(bounds[:-1], bounds[1:]):
        cs = slice(c0, c1)
        parts = halves if c0 == 0 or c1 == d_ff else whole
        f = jnp.concatenate([ffn_chunk(rs, cs, f[rs]) for rs in parts], axis=0)
        gate_ref[:SUBLANES, cs] = gate_ref[rows:rows + SUBLANES, cs]
    for rs in halves:
        out_ref[0, rs] = _layer_norm(alpha * y1[rs] + f[rs], g2_ref[...], b2_ref[...])


def _rope_tables(seq_len):
    pos = jnp.arange(seq_len, dtype=F32)
    inv_freq = 1.0 / (ROPE_THETA ** (jnp.arange(0, DIFF_HEAD_DIM, 2, dtype=F32) / DIFF_HEAD_DIM))
    ang = pos[:, None] * inv_freq[None, :]
    cos, sin = jnp.cos(ang), jnp.sin(ang)
    reps = LANES // (DIFF_HEAD_DIM // 2)
    cos_t = jnp.tile(cos, (1, reps))
    sin_t = jnp.tile(jnp.concatenate([-sin, sin], axis=1), (1, reps // 2))
    return cos_t, sin_t


def _const_spec(shape):
    return pl.BlockSpec(shape, lambda *_: (0,) * len(shape), pipeline_mode=pl.Buffered(1))


def _layer(x, cos_t, sin_t, lambda_init, alpha, w_in, lq1, lk1, lq2, lk2, subln_g, lng, lnb,
           w_sp, b_sp, w_out, ln1_g, ln1_b, w_gate, w_up, conv_w, conv_b, w_down, ln2_g, ln2_b):
    B, S, D = x.shape
    att_width = N_DIFF_HEADS * PAIR_DIM
    gmlp_width = N_GMLP_GROUPS * GMLP_GROUP_DIM
    d_ff = w_gate.shape[1]
    n_pairs = N_DIFF_HEADS
    n_groups = 2 * IN_ROWS // ATT_COLS
    row = lambda a: a.reshape(1, -1)

    q, k, v, ob = pl.pallas_call(
        functools.partial(_in_proj_kernel, att_width=att_width, gmlp_width=gmlp_width),
        grid=(B, S // IN_ROWS),
        in_specs=[
            pl.BlockSpec((1, IN_ROWS, D), lambda b, i: (b, i, 0)),
            _const_spec(w_in.shape),
            pl.BlockSpec((IN_ROWS, LANES), lambda b, i: (i, 0)),
            pl.BlockSpec((IN_ROWS, LANES), lambda b, i: (i, 0)),
            _const_spec((1, gmlp_width)),
            _const_spec((1, gmlp_width)),
            _const_spec(w_sp.shape),
            _const_spec((GMLP_CHUNK, N_GMLP_GROUPS)),
        ],
        out_specs=[
            pl.BlockSpec((1, 2 * n_pairs, IN_ROWS, PAIR_DIM), lambda b, i: (b, 0, i, 0)),
            pl.BlockSpec((1, n_pairs, IN_ROWS, PAIR_DIM), lambda b, i: (b, 0, i, 0)),
            pl.BlockSpec((1, n_pairs, 1, VT_ROWS, IN_ROWS), lambda b, i: (b, 0, i, 0, 0)),
            pl.BlockSpec((1, IN_ROWS, gmlp_width), lambda b, i: (b, i, 0)),
        ],
        out_shape=[
            jax.ShapeDtypeStruct((B, 2 * n_pairs, S, PAIR_DIM), BF16),
            jax.ShapeDtypeStruct((B, n_pairs, S, PAIR_DIM), BF16),
            jax.ShapeDtypeStruct((B, n_pairs, S // IN_ROWS, VT_ROWS, IN_ROWS), BF16),
            jax.ShapeDtypeStruct((B, S, gmlp_width), BF16),
        ],
        scratch_shapes=[pltpu.VMEM(w_in.shape, BF16)],
        compiler_params=pltpu.CompilerParams(
            dimension_semantics=("arbitrary", "arbitrary"), vmem_limit_bytes=VMEM_LIMIT),
        name="in_proj_gmlp",
    )(x, w_in, cos_t, sin_t, row(lng), row(lnb), w_sp, b_sp.T)

    n_steps = B * n_pairs

    def slab_spec(w):
        slab = next(s for s in range(BF16_SUBLANES, w.shape[0] + 1, BF16_SUBLANES)
                    if w.shape[0] % s == 0 and w.shape[0] // s <= n_steps)
        last = w.shape[0] // slab - 1
        return pl.BlockSpec((slab, w.shape[1]),
                            lambda b, p: (jnp.minimum(b * n_pairs + p, last), 0))

    ffn_weights = (w_out, w_gate, w_up, w_down)
    oa, w_out16, w_gate16, w_up16, w_down16 = pl.pallas_call(
        functools.partial(_attn_kernel, lambda_init=lambda_init),
        grid=(B, n_pairs),
        in_specs=[
            pl.BlockSpec((1, 2, S, PAIR_DIM), lambda b, p: (b, p, 0, 0)),
            pl.BlockSpec((1, 1, S, PAIR_DIM), lambda b, p: (b, p, 0, 0)),
            pl.BlockSpec((1, 1, S // IN_ROWS, VT_ROWS, IN_ROWS), lambda b, p: (b, p, 0, 0, 0)),
            _const_spec((1, DIFF_HEAD_DIM)),
            _const_spec((1, DIFF_HEAD_DIM)),
            _const_spec((1, DIFF_HEAD_DIM)),
            _const_spec((1, DIFF_HEAD_DIM)),
            _const_spec((1, PAIR_DIM)),
        ] + [slab_spec(w) for w in ffn_weights],
        out_specs=[pl.BlockSpec((1, S, PAIR_DIM), lambda b, p: (b, 0, p))]
        + [slab_spec(w) for w in ffn_weights],
        out_shape=[jax.ShapeDtypeStruct((B, S, att_width), BF16)]
        + [jax.ShapeDtypeStruct(w.shape, BF16) for w in ffn_weights],
        scratch_shapes=[
            pltpu.VMEM((1, 2 * IN_ROWS), F32),
            pltpu.VMEM((n_groups, VT_ROWS, ATT_COLS), F32),
            pltpu.VMEM((n_groups, IN_ROWS, ATT_COLS), F32),
            pltpu.VMEM((n_groups, IN_ROWS, ATT_COLS), F32),
            pltpu.VMEM((n_groups, IN_ROWS, ATT_COLS), F32),
            pltpu.VMEM((1, 2 * IN_ROWS), F32),
            pltpu.VMEM((1, 2 * IN_ROWS), F32),
            pltpu.VMEM((1, 2 * IN_ROWS), F32),
        ],
        compiler_params=pltpu.CompilerParams(
            dimension_semantics=("arbitrary", "arbitrary"),
            vmem_limit_bytes=VMEM_LIMIT),
        name="diff_attention",
    )(q, k, v, row(lq1), row(lk1), row(lq2), row(lk2), row(subln_g), *ffn_weights)

    out = pl.pallas_call(
        functools.partial(_out_ffn_kernel, alpha=alpha),
        grid=(B, S // FFN_ROWS),
        in_specs=[
            pl.BlockSpec((1, FFN_ROWS, D), lambda b, i: (b, i, 0)),
            pl.BlockSpec((1, FFN_ROWS, att_width), lambda b, i: (b, i, 0)),
            pl.BlockSpec((1, FFN_ROWS, gmlp_width), lambda b, i: (b, i, 0)),
            _const_spec(w_out.shape),
            _const_spec((1, D)),
            _const_spec((1, D)),
            _const_spec(w_gate.shape),
            _const_spec(w_up.shape),
            _const_spec(conv_w.shape),
            _const_spec((1, d_ff)),
            _const_spec(w_down.shape),
            _const_spec((1, D)),
            _const_spec((1, D)),
        ],
        out_specs=pl.BlockSpec((1, FFN_ROWS, D), lambda b, i: (b, i, 0)),
        out_shape=jax.ShapeDtypeStruct((B, S, D), x.dtype),
        scratch_shapes=[pltpu.VMEM((SUBLANES + FFN_ROWS, d_ff), F32)],
        compiler_params=pltpu.CompilerParams(
            dimension_semantics=("arbitrary", "arbitrary"), vmem_limit_bytes=VMEM_LIMIT),
        name="out_proj_ffn",
    )(x, oa, ob, w_out16, row(ln1_g), row(ln1_b), w_gate16, w_up16, conv_w, row(conv_b),
      w_down16, row(ln2_g), row(ln2_b))
    return out


def kernel(x, w_in, lambda_q1, lambda_k1, lambda_q2, lambda_k2, subln_g, gmlp_ln_g, gmlp_ln_b,
           w_spatial, b_spatial, w_out, ln1_g, ln1_b, w_gate, w_up, conv_w, conv_b, w_down,
           ln2_g, ln2_b):
    depth = w_in.shape[0]
    alpha = (2 * depth) ** 0.25
    cos_t, sin_t = _rope_tables(x.shape[1])
    for l in range(depth):
        lambda_init = 0.8 - 0.6 * math.exp(-0.3 * l)
        x = _layer(x, cos_t, sin_t, lambda_init, alpha, w_in[l], lambda_q1[l], lambda_k1[l],
                   lambda_q2[l], lambda_k2[l], subln_g[l], gmlp_ln_g[l], gmlp_ln_b[l],
                   w_spatial[l], b_spatial[l], w_out[l], ln1_g[l], ln1_b[l], w_gate[l],
                   w_up[l], conv_w[l], conv_b[l], w_down[l], ln2_g[l], ln2_b[l])
    return x
```

```python
import functools
import math
from typing import Any, NamedTuple

import jax
import jax.numpy as jnp
from jax import lax
from jax.experimental import pallas as pl
from jax.experimental.pallas import tpu as pltpu

F32 = jnp.float32
BF16 = jnp.bfloat16

CHUNK = 64
N_DIFF_HEADS = 4
DIFF_HEAD_DIM = 64
PAIR_DIM = 2 * DIFF_HEAD_DIM
N_GMLP_GROUPS = 4
GMLP_GROUP_DIM = 128
GMLP_CHUNK = 128
CONV_WIDTH = 3
ROPE_THETA = 10000.0
LN_EPS = 1e-5

LANES = 128
SUBLANES = 8
BF16_SUBLANES = 16
VT_ROWS = PAIR_DIM + BF16_SUBLANES
LOG2_E = math.log2(math.e)

IN_ROWS = 512
ATT_COLS = 256
ATT_UNROLL = 6
FFN_ROWS = 512
FFN_FIRST_COLS = 1024
FFN_LAST_COLS = 768
MASKED_SCORE = -1e30
VMEM_LIMIT = 56 * 1024 * 1024


class _Tile(NamedTuple):
    value: Any
    parity: int


def _layer_norm(z, g, b):
    mu = jnp.mean(z, axis=-1, keepdims=True)
    d = z - mu
    var = jnp.mean(d * d, axis=-1, keepdims=True)
    return d * lax.rsqrt(var + LN_EPS) * g + b


def _gelu(z):
    return 0.5 * z * (1.0 + lax.erf(z * (1.0 / math.sqrt(2.0))))


def _in_proj_kernel(x_ref, w_ref, cos_ref, sin_ref, lng_ref, lnb_ref, wsp_ref, bsp_ref,
                    q_ref, k_ref, vt_ref, ob_ref, w16_ref, *, att_width, gmlp_width):
    @pl.when((pl.program_id(0) == 0) & (pl.program_id(1) == 0))
    def _():
        w16_ref[...] = w_ref[...].astype(BF16)

    rows = x_ref.shape[1]
    xb = x_ref[0].astype(BF16)

    def proj(col0, width):
        return jnp.dot(xb, w16_ref[:, col0:col0 + width], preferred_element_type=F32)

    cos = cos_ref[...]
    sin = sin_ref[...]
    lane = lax.broadcasted_iota(jnp.int32, (rows, LANES), 1)
    first_half = (lane & (DIFF_HEAD_DIM // 2)) == 0
    low_head = lane < DIFF_HEAD_DIM

    def rope(t):
        rot = jnp.where(first_half, pltpu.roll(t, LANES - DIFF_HEAD_DIM // 2, 1),
                        pltpu.roll(t, DIFF_HEAD_DIM // 2, 1))
        return t * cos + rot * sin

    vg = _gelu(proj(3 * att_width + gmlp_width, gmlp_width))
    u = _gelu(proj(3 * att_width, gmlp_width))
    n_chunks = rows // GMLP_CHUNK
    gate_rhs = []
    for g in range(N_GMLP_GROUPS):
        gs = slice(g * GMLP_GROUP_DIM, (g + 1) * GMLP_GROUP_DIM)
        vn = _layer_norm(vg[:, gs], lng_ref[:, gs], lnb_ref[:, gs]).astype(BF16)
        gate_rhs.append(jnp.concatenate(
            [vn[c * GMLP_CHUNK:(c + 1) * GMLP_CHUNK] for c in range(n_chunks)], axis=1))

    n_pairs = att_width // PAIR_DIM
    hq = proj(0, att_width)
    for p in range(n_pairs):
        r = rope(hq[:, p * PAIR_DIM:(p + 1) * PAIR_DIM]) * (DIFF_HEAD_DIM ** -0.5 * LOG2_E)
        q_ref[0, 2 * p] = jnp.where(low_head, r, 0.0).astype(BF16)
        q_ref[0, 2 * p + 1] = jnp.where(low_head, 0.0, r).astype(BF16)
    hk = proj(att_width, att_width)
    for p in range(n_pairs):
        k_ref[0, p] = rope(hk[:, p * PAIR_DIM:(p + 1) * PAIR_DIM]).astype(BF16)
    hv = proj(2 * att_width, att_width)
    for p in range(n_pairs):
        vt_ref[0, p, 0, :PAIR_DIM, :] = hv[:, p * PAIR_DIM:(p + 1) * PAIR_DIM].T.astype(BF16)
        extra = lax.broadcasted_iota(jnp.int32, (BF16_SUBLANES, rows), 0)
        vt_ref[0, p, 0, PAIR_DIM:, :] = jnp.where(extra == 0, 1.0, 0.0).astype(BF16)

    t_idx = lax.broadcasted_iota(jnp.int32, (GMLP_CHUNK, GMLP_CHUNK), 0)
    s_idx = lax.broadcasted_iota(jnp.int32, (GMLP_CHUNK, GMLP_CHUNK), 1)
    for g in range(N_GMLP_GROUPS):
        gs = slice(g * GMLP_GROUP_DIM, (g + 1) * GMLP_GROUP_DIM)
        w_causal = jnp.where(t_idx >= s_idx, wsp_ref[g], 0.0).astype(BF16)
        gate = jnp.dot(w_causal, gate_rhs[g], preferred_element_type=F32)
        bias = bsp_ref[:, g:g + 1]
        for c in range(n_chunks):
            rs = slice(c * GMLP_CHUNK, (c + 1) * GMLP_CHUNK)
            gate_c = gate[:, c * GMLP_GROUP_DIM:(c + 1) * GMLP_GROUP_DIM] + bias
            ob_ref[0, rs, gs] = (u[rs, gs] * gate_c).astype(BF16)


def _attn_kernel(q_ref, k_ref, vt_ref, lq1_ref, lk1_ref, lq2_ref, lk2_ref, g_ref,
                 wo_ref, wg_ref, wu_ref, wd_ref,
                 o_ref, wo16_ref, wg16_ref, wu16_ref, wd16_ref,
                 m_ref, acc_ref, sa_ref, sb_ref, sc_ref, ma_ref, mb_ref, mc_ref,
                 *, lambda_init):
    for src, dst in ((wo_ref, wo16_ref), (wg_ref, wg16_ref), (wu_ref, wu16_ref),
                     (wd_ref, wd16_ref)):
        dst[...] = src[...].astype(BF16)

    tk = vt_ref.shape[4]
    tq = tk
    n_q = q_ref.shape[2] // tq
    groups = range(0, 2 * tq, ATT_COLS)

    def live_keys(c0, diagonal):
        return min(tk, (c0 & (tq - 1)) + ATT_COLS) if diagonal else tk

    def scores(qi, kt, s_ref, mx_ref, diagonal, only=None):
        q = q_ref[0, :, pl.ds(pl.multiple_of(qi * tq, tq), tq), :].reshape(2 * tq, PAIR_DIM)
        k = k_ref[0, 0, pl.ds(pl.multiple_of(kt * tk, tk), tk), :]
        for g, c0 in enumerate(groups):
            if only is not None and g != only:
                continue
            cols = slice(c0, c0 + ATT_COLS)
            n_keys = live_keys(c0, diagonal)
            s = lax.dot_general(k[:n_keys], q[cols], (((1,), (1,)), ((), ())),
                                preferred_element_type=F32)
            if diagonal:
                k_chunk = lax.broadcasted_iota(jnp.int32, s.shape, 0) // CHUNK
                q_pos = lax.broadcasted_iota(jnp.int32, s.shape, 1) + (c0 & (tq - 1))
                s = jnp.where(k_chunk <= q_pos // CHUNK, s, MASKED_SCORE)
            s_ref[g, :n_keys] = s
            mx_ref[:, cols] = jnp.max(s, axis=0, keepdims=True)

    def update(kt, s_ref, mx_ref, diagonal, only=None):
        vt = vt_ref[0, 0, kt]
        for g, c0 in enumerate(groups):
            if only is not None and g != only:
                continue
            cols = slice(c0, c0 + ATT_COLS)
            n_keys = live_keys(c0, diagonal)
            m_prev = m_ref[:, cols]
            m_new = jnp.maximum(m_prev, mx_ref[:, cols])
            alpha = jnp.exp2(m_prev - m_new)
            p = jnp.exp2(s_ref[g, :n_keys] - m_new)
            acc_ref[g] = alpha * acc_ref[g] + jnp.dot(
                vt[:, :n_keys], p.astype(BF16), preferred_element_type=F32)
            m_ref[:, cols] = m_new

    def start_tile():
        m_ref[...] = jnp.full(m_ref.shape, MASKED_SCORE, F32)
        acc_ref[...] = jnp.zeros(acc_ref.shape, F32)

    lam = (jnp.exp(jnp.sum(lq1_ref[...] * lk1_ref[...], axis=1, keepdims=True))
           - jnp.exp(jnp.sum(lq2_ref[...] * lk2_ref[...], axis=1, keepdims=True))
           + lambda_init)

    def finish_tile(qi):
        acc = jnp.concatenate([acc_ref[g] for g in range(len(groups))], axis=1)
        o_all = acc[:PAIR_DIM] / acc[PAIR_DIM:PAIR_DIM + 1]
        o = o_all[:, :tq] - lam * o_all[:, tq:]
        o = o * lax.rsqrt(jnp.mean(o * o, axis=0, keepdims=True) + LN_EPS)
        o = o.T * (g_ref[...] * (1.0 - lambda_init))
        o_ref[0, pl.ds(pl.multiple_of(qi * tq, tq), tq), :] = o.astype(o_ref.dtype)

    start_tile()
    scores(0, 0, sb_ref, mb_ref, True)
    second = min(1, n_q - 1)
    for g in range(len(groups)):
        scores(second, second, sc_ref, mc_ref, True, only=g)
        update(0, sb_ref, mb_ref, True, only=g)

    def query_tile(i, carry):
        nxt = jnp.minimum(i + 1, n_q - 1)
        finish_tile(i - 1)
        start_tile()
        for g in range(len(groups)):
            scores(i, 0, sa_ref, ma_ref, False, only=g)
            update(i, sc_ref, mc_ref, True, only=g)
        bufs = ((sa_ref, ma_ref), (sb_ref, mb_ref))

        def next_scores_then_update(j):
            for g in range(len(groups)):
                scores(i, j.value + 1, *bufs[(j.parity + 1) % 2], False, only=g)
                update(j.value, *bufs[j.parity], False, only=g)

        def unrolled(t, c):
            for u in range(ATT_UNROLL):
                next_scores_then_update(_Tile(ATT_UNROLL * t + u, u % 2))
            return c

        done = (i - 1) // ATT_UNROLL
        lax.fori_loop(0, done, unrolled, 0)

        base = ATT_UNROLL * done
        for left in range(ATT_UNROLL):
            @pl.when((i - 1) % ATT_UNROLL == left)
            def _(left=left):
                for u in range(left):
                    next_scores_then_update(_Tile(base + u, u % 2))
                for g in range(len(groups)):
                    scores(nxt, nxt, sc_ref, mc_ref, True, only=g)
                    update(base + left, *bufs[left % 2], False, only=g)

        return carry

    lax.fori_loop(1, n_q, query_tile, 0)
    finish_tile(n_q - 1)


def _out_ffn_kernel(x_ref, oa_ref, ob_ref, wo_ref, g1_ref, b1_ref, wg_ref, wu_ref,
                    cw_ref, cb_ref, wd_ref, g2_ref, b2_ref, out_ref, gate_ref, *, alpha):
    rows = x_ref.shape[1]
    d_ff = wg_ref.shape[1]
    seq_tile = pl.program_id(1)
    halves = [slice(0, rows // 2), slice(rows // 2, rows)]
    whole = [slice(0, rows)]

    def mixed(rs):
        cat = jnp.concatenate([oa_ref[0, rs], ob_ref[0, rs]], axis=1)
        mix = jnp.dot(cat, wo_ref[...], preferred_element_type=F32)
        return _layer_norm(alpha * x_ref[0, rs] + mix, g1_ref[...], b1_ref[...])

    y1 = jnp.concatenate([mixed(rs) for rs in halves], axis=0)
    yb = y1.astype(BF16)

    @pl.when(seq_tile == 0)
    def _():
        gate_ref[:SUBLANES, :] = jnp.zeros((SUBLANES, d_ff), F32)

    def ffn_chunk(rs, cs, f_rows):
        g = jnp.dot(yb[rs], wg_ref[:, cs], preferred_element_type=F32)
        up = jnp.dot(yb[rs], wu_ref[:, cs], preferred_element_type=F32)
        gate_ref[SUBLANES + rs.start:SUBLANES + rs.stop, cs] = g
        conv = cb_ref[:, cs] + cw_ref[CONV_WIDTH - 1:CONV_WIDTH, cs] * g
        for tap in range(CONV_WIDTH - 1):
            back = CONV_WIDTH - 1 - tap
            window = slice(SUBLANES + rs.start - back, SUBLANES + rs.stop - back)
            conv = conv + cw_ref[tap:tap + 1, cs] * gate_ref[window, cs]
        half = 0.5 * conv
        h = ((half + half * jnp.tanh(half)) * up).astype(BF16)
        return f_rows + jnp.dot(h, wd_ref[cs, :], preferred_element_type=F32)

    f = jnp.zeros((rows, x_ref.shape[2]), F32)
    bounds = [0, FFN_FIRST_COLS, d_ff - FFN_LAST_COLS, d_ff]
    for c0, c1 in zip(bounds[:-1], bounds[1:]):
        cs = slice(c0, c1)
        parts = halves if c0 == 0 or c1 == d_ff else whole
        f = jnp.concatenate([ffn_chunk(rs, cs, f[rs]) for rs in parts], axis=0)
        gate_ref[:SUBLANES, cs] = gate_ref[rows:rows + SUBLANES, cs]
    for rs in halves:
        out_ref[0, rs] = _layer_norm(alpha * y1[rs] + f[rs], g2_ref[...], b2_ref[...])


def _rope_tables(seq_len):
    pos = jnp.arange(seq_len, dtype=F32)
    inv_freq = 1.0 / (ROPE_THETA ** (jnp.arange(0, DIFF_HEAD_DIM, 2, dtype=F32) / DIFF_HEAD_DIM))
    ang = pos[:, None] * inv_freq[None, :]
    cos, sin = jnp.cos(ang), jnp.sin(ang)
    reps = LANES // (DIFF_HEAD_DIM // 2)
    cos_t = jnp.tile(cos, (1, reps))
    sin_t = jnp.tile(jnp.concatenate([-sin, sin], axis=1), (1, reps // 2))
    return cos_t, sin_t


def _const_spec(shape):
    return pl.BlockSpec(shape, lambda *_: (0,) * len(shape), pipeline_mode=pl.Buffered(1))


def _layer(x, cos_t, sin_t, lambda_init, alpha, w_in, lq1, lk1, lq2, lk2, subln_g, lng, lnb,
           w_sp, b_sp, w_out, ln1_g, ln1_b, w_gate, w_up, conv_w, conv_b, w_down, ln2_g, ln2_b):
    B, S, D = x.shape
    att_width = N_DIFF_HEADS * PAIR_DIM
    gmlp_width = N_GMLP_GROUPS * GMLP_GROUP_DIM
    d_ff = w_gate.shape[1]
    n_pairs = N_DIFF_HEADS
    n_groups = 2 * IN_ROWS // ATT_COLS
    row = lambda a: a.reshape(1, -1)

    q, k, v, ob = pl.pallas_call(
        functools.partial(_in_proj_kernel, att_width=att_width, gmlp_width=gmlp_width),
        grid=(B, S // IN_ROWS),
        in_specs=[
            pl.BlockSpec((1, IN_ROWS, D), lambda b, i: (b, i, 0)),
            _const_spec(w_in.shape),
            pl.BlockSpec((IN_ROWS, LANES), lambda b, i: (i, 0)),
            pl.BlockSpec((IN_ROWS, LANES), lambda b, i: (i, 0)),
            _const_spec((1, gmlp_width)),
            _const_spec((1, gmlp_width)),
            _const_spec(w_sp.shape),
            _const_spec((GMLP_CHUNK, N_GMLP_GROUPS)),
        ],
        out_specs=[
            pl.BlockSpec((1, 2 * n_pairs, IN_ROWS, PAIR_DIM), lambda b, i: (b, 0, i, 0)),
            pl.BlockSpec((1, n_pairs, IN_ROWS, PAIR_DIM), lambda b, i: (b, 0, i, 0)),
            pl.BlockSpec((1, n_pairs, 1, VT_ROWS, IN_ROWS), lambda b, i: (b, 0, i, 0, 0)),
            pl.BlockSpec((1, IN_ROWS, gmlp_width), lambda b, i: (b, i, 0)),
        ],
        out_shape=[
            jax.ShapeDtypeStruct((B, 2 * n_pairs, S, PAIR_DIM), BF16),
            jax.ShapeDtypeStruct((B, n_pairs, S, PAIR_DIM), BF16),
            jax.ShapeDtypeStruct((B, n_pairs, S // IN_ROWS, VT_ROWS, IN_ROWS), BF16),
            jax.ShapeDtypeStruct((B, S, gmlp_width), BF16),
        ],
        scratch_shapes=[pltpu.VMEM(w_in.shape, BF16)],
        compiler_params=pltpu.CompilerParams(
            dimension_semantics=("arbitrary", "arbitrary"), vmem_limit_bytes=VMEM_LIMIT),
        name="in_proj_gmlp",
    )(x, w_in, cos_t, sin_t, row(lng), row(lnb), w_sp, b_sp.T)

    n_steps = B * n_pairs

    def slab_spec(w):
        slab = next(s for s in range(BF16_SUBLANES, w.shape[0] + 1, BF16_SUBLANES)
                    if w.shape[0] % s == 0 and w.shape[0] // s <= n_steps)
        last = w.shape[0] // slab - 1
        return pl.BlockSpec((slab, w.shape[1]),
                            lambda b, p: (jnp.minimum(b * n_pairs + p, last), 0))

    ffn_weights = (w_out, w_gate, w_up, w_down)
    oa, w_out16, w_gate16, w_up16, w_down16 = pl.pallas_call(
        functools.partial(_attn_kernel, lambda_init=lambda_init),
        grid=(B, n_pairs),
        in_specs=[
            pl.BlockSpec((1, 2, S, PAIR_DIM), lambda b, p: (b, p, 0, 0)),
            pl.BlockSpec((1, 1, S, PAIR_DIM), lambda b, p: (b, p, 0, 0)),
            pl.BlockSpec((1, 1, S // IN_ROWS, VT_ROWS, IN_ROWS), lambda b, p: (b, p, 0, 0, 0)),
            _const_spec((1, DIFF_HEAD_DIM)),
            _const_spec((1, DIFF_HEAD_DIM)),
            _const_spec((1, DIFF_HEAD_DIM)),
            _const_spec((1, DIFF_HEAD_DIM)),
            _const_spec((1, PAIR_DIM)),
        ] + [slab_spec(w) for w in ffn_weights],
        out_specs=[pl.BlockSpec((1, S, PAIR_DIM), lambda b, p: (b, 0, p))]
        + [slab_spec(w) for w in ffn_weights],
        out_shape=[jax.ShapeDtypeStruct((B, S, att_width), BF16)]
        + [jax.ShapeDtypeStruct(w.shape, BF16) for w in ffn_weights],
        scratch_shapes=[
            pltpu.VMEM((1, 2 * IN_ROWS), F32),
            pltpu.VMEM((n_groups, VT_ROWS, ATT_COLS), F32),
            pltpu.VMEM((n_groups, IN_ROWS, ATT_COLS), F32),
            pltpu.VMEM((n_groups, IN_ROWS, ATT_COLS), F32),
            pltpu.VMEM((n_groups, IN_ROWS, ATT_COLS), F32),
            pltpu.VMEM((1, 2 * IN_ROWS), F32),
            pltpu.VMEM((1, 2 * IN_ROWS), F32),
            pltpu.VMEM((1, 2 * IN_ROWS), F32),
        ],
        compiler_params=pltpu.CompilerParams(
            dimension_semantics=("arbitrary", "arbitrary"),
            vmem_limit_bytes=VMEM_LIMIT),
        name="diff_attention",
    )(q, k, v, row(lq1), row(lk1), row(lq2), row(lk2), row(subln_g), *ffn_weights)

    out = pl.pallas_call(
        functools.partial(_out_ffn_kernel, alpha=alpha),
        grid=(B, S // FFN_ROWS),
        in_specs=[
            pl.BlockSpec((1, FFN_ROWS, D), lambda b, i: (b, i, 0)),
            pl.BlockSpec((1, FFN_ROWS, att_width), lambda b, i: (b, i, 0)),
            pl.BlockSpec((1, FFN_ROWS, gmlp_width), lambda b, i: (b, i, 0)),
            _const_spec(w_out.shape),
            _const_spec((1, D)),
            _const_spec((1, D)),
            _const_spec(w_gate.shape),
            _const_spec(w_up.shape),
            _const_spec(conv_w.shape),
            _const_spec((1, d_ff)),
            _const_spec(w_down.shape),
            _const_spec((1, D)),
            _const_spec((1, D)),
        ],
        out_specs=pl.BlockSpec((1, FFN_ROWS, D), lambda b, i: (b, i, 0)),
        out_shape=jax.ShapeDtypeStruct((B, S, D), x.dtype),
        scratch_shapes=[pltpu.VMEM((SUBLANES + FFN_ROWS, d_ff), F32)],
        compiler_params=pltpu.CompilerParams(
            dimension_semantics=("arbitrary", "arbitrary"), vmem_limit_bytes=VMEM_LIMIT),
        name="out_proj_ffn",
    )(x, oa, ob, w_out16, row(ln1_g), row(ln1_b), w_gate16, w_up16, conv_w, row(conv_b),
      w_down16, row(ln2_g), row(ln2_b))
    return out


def kernel(x, w_in, lambda_q1, lambda_k1, lambda_q2, lambda_k2, subln_g, gmlp_ln_g, gmlp_ln_b,
           w_spatial, b_spatial, w_out, ln1_g, ln1_b, w_gate, w_up, conv_w, conv_b, w_down,
           ln2_g, ln2_b):
    depth = w_in.shape[0]
    alpha = (2 * depth) ** 0.25
    cos_t, sin_t = _rope_tables(x.shape[1])
    for l in range(depth):
        lambda_init = 0.8 - 0.6 * math.exp(-0.3 * l)
        x = _layer(x, cos_t, sin_t, lambda_init, alpha, w_in[l], lambda_q1[l], lambda_k1[l],
                   lambda_q2[l], lambda_k2[l], subln_g[l], gmlp_ln_g[l], gmlp_ln_b[l],
                   w_spatial[l], b_spatial[l], w_out[l], ln1_g[l], ln1_b[l], w_gate[l],
                   w_up[l], conv_w[l], conv_b[l], w_down[l], ln2_g[l], ln2_b[l])
    return x
```
